```python
import functools
import jax, jax.numpy as jnp
from jax import lax
import numpy as np

D_MODEL = 2048
BATCH = 32
SEQ = 256
DEPTH = 4
DEC_BATCH = 4
DEC_SEQ = 4096
PAST_LEN = 256

GRID_W = 64
NA_HEADS = 16
NA_HEAD_DIM = 128
NA_WIDTH = NA_HEADS * NA_HEAD_DIM
NA_WIN_ROWS = 8
NA_WIN_COLS = 16
GMLP_GROUPS = 8
GMLP_WIDTH = 2048
GMLP_CHUNK = 128
N_EXPERTS = 16
N_EXPERT_GROUPS = 4
EXPERTS_PER_GROUP = N_EXPERTS // N_EXPERT_GROUPS
TOP_K = 2
D_EXPERT = 1024
MOE_BLOCK = 256
N_MOD = 6
EPS = 1e-6
ATTN_Q_BLOCK = 128
IN_WIDTH = 3 * NA_WIDTH + 2 * GMLP_WIDTH + 2 * D_MODEL
SPLIT_POINTS = (NA_WIDTH, 2 * NA_WIDTH, 3 * NA_WIDTH,
                3 * NA_WIDTH + GMLP_WIDTH, 3 * NA_WIDTH + 2 * GMLP_WIDTH,
                3 * NA_WIDTH + 2 * GMLP_WIDTH + D_MODEL)

kernel_name = "hybrid_dit_gmlp_natten_moe_step"


def _rmsnorm(x, g):
    xf = x.astype(jnp.float32)
    y = xf * lax.rsqrt(jnp.mean(xf * xf, axis=-1, keepdims=True) + EPS)
    return (y * g.astype(jnp.float32)).astype(x.dtype)


def _modulation(cond, w_ada_l, b_ada_l):
    m = jax.nn.silu(cond) @ w_ada_l + b_ada_l
    return m.reshape(cond.shape[0], N_MOD, D_MODEL)


def _gmlp_spatial_gating(u, v, gmlp_g, w_s, b_s):
    B, S, _ = u.shape
    u = jax.nn.gelu(u)
    v = _rmsnorm(jax.nn.gelu(v), gmlp_g)
    vc = v.reshape(B, S // GMLP_CHUNK, GMLP_CHUNK, GMLP_GROUPS, GMLP_WIDTH // GMLP_GROUPS)
    s = jnp.einsum('gpq,bnqgc->bnpgc', w_s, vc) + b_s.T[None, None, :, :, None]
    return u * s.reshape(B, S, GMLP_WIDTH)


def _context_attention(q, k, v):
    B, S, H, Dh = q.shape
    qb = (q * (Dh ** -0.5)).reshape(B, S // ATTN_Q_BLOCK, ATTN_Q_BLOCK, H, Dh).transpose(1, 0, 2, 3, 4)

    def one_block(qi):
        s = jnp.einsum('bqhd,bkhd->bhqk', qi, k).astype(jnp.float32)
        p = jax.nn.softmax(s, axis=-1).astype(v.dtype)
        return jnp.einsum('bhqk,bkhd->bqhd', p, v)

    o = lax.map(one_block, qb)
    return o.transpose(1, 0, 2, 3, 4).reshape(B, S, H * Dh)


def _neighbourhood_attention(q, k, v, k_ctx, v_ctx, rpb):
    B, T, H, Dh = q.shape
    rows = T // GRID_W
    kh = min(NA_WIN_ROWS, rows)
    qg = (q * (Dh ** -0.5)).reshape(B, rows, GRID_W, H, Dh)
    kg = k.reshape(B, rows, GRID_W, H, Dh)
    vg = v.reshape(B, rows, GRID_W, H, Dh)
    col = jnp.arange(GRID_W)
    cs = jnp.clip(col - NA_WIN_COLS // 2, 0, GRID_W - NA_WIN_COLS)
    col_valid = (col[None, :] >= cs[:, None]) & (col[None, :] < cs[:, None] + NA_WIN_COLS)
    col_idx = jnp.clip(col[None, :] - col[:, None] + NA_WIN_COLS - 1, 0, 2 * NA_WIN_COLS - 2)
    mask = jnp.tile(col_valid, (1, kh))

    def one_row(r):
        rs = jnp.clip(r - kh // 2, 0, rows - kh)
        q_r = lax.dynamic_index_in_dim(qg, r, axis=1, keepdims=False)
        k_band = lax.dynamic_slice_in_dim(kg, rs, kh, axis=1).reshape(B, kh * GRID_W, H, Dh)
        v_band = lax.dynamic_slice_in_dim(vg, rs, kh, axis=1).reshape(B, kh * GRID_W, H, Dh)
        dr = rs + jnp.arange(kh) - r + NA_WIN_ROWS - 1
        bias = rpb[:, dr][:, :, col_idx]
        bias = bias.transpose(0, 2, 1, 3).reshape(H, GRID_W, kh * GRID_W).astype(jnp.float32)
        s_loc = jnp.einsum('bqhd,bkhd->bhqk', q_r, k_band).astype(jnp.float32) + bias
        s_loc = jnp.where(mask, s_loc, -jnp.inf)
        s_ctx = jnp.einsum('bqhd,bkhd->bhqk', q_r, k_ctx).astype(jnp.float32)
        p = jax.nn.softmax(jnp.concatenate([s_loc, s_ctx], axis=-1), axis=-1).astype(v.dtype)
        n_loc = kh * GRID_W
        return (jnp.einsum('bhqk,bkhd->bqhd', p[..., :n_loc], v_band)
                + jnp.einsum('bhqk,bkhd->bqhd', p[..., n_loc:], v_ctx))

    o = lax.map(one_row, jnp.arange(rows))
    return o.transpose(1, 0, 2, 3, 4).reshape(B, T, H * Dh)


def _token_mixing(h, w_in, gmlp_g, w_s, b_s, w_proj_a, w_proj_b, w_out, attend):
    B, S, _ = h.shape
    proj = jnp.einsum('bsd,de->bse', h, w_in)
    q, k, v, u, gv, ga, gb = jnp.split(proj, SPLIT_POINTS, axis=-1)
    q = q.reshape(B, S, NA_HEADS, NA_HEAD_DIM)
    k = k.reshape(B, S, NA_HEADS, NA_HEAD_DIM)
    v = v.reshape(B, S, NA_HEADS, NA_HEAD_DIM)
    y_a = _gmlp_spatial_gating(u, gv, gmlp_g, w_s, b_s)
    y_b = attend(q, k, v)
    merged = (jax.nn.sigmoid(ga) * jnp.einsum('bse,ed->bsd', y_a, w_proj_a)
              + jax.nn.sigmoid(gb) * jnp.einsum('bse,ed->bsd', y_b, w_proj_b))
    return jnp.einsum('bsd,de->bse', merged, w_out), k, v


def _route(x, router_w, router_b):
    T = x.shape[0]
    probs = jax.nn.softmax((x @ router_w).astype(jnp.float32), axis=-1)
    sel = (probs + router_b.astype(jnp.float32)).reshape(T, N_EXPERT_GROUPS, EXPERTS_PER_GROUP)
    group_score = lax.top_k(sel, TOP_K)[0].sum(-1)
    best_g = jnp.argmax(group_score, axis=-1)
    in_group = jnp.take_along_axis(sel, best_g[:, None, None], axis=1)[:, 0]
    _, local = lax.top_k(in_group, TOP_K)
    experts = (best_g[:, None] * EXPERTS_PER_GROUP + local).astype(jnp.int32)
    w = jnp.take_along_axis(probs, experts, axis=-1)
    return experts, w / jnp.sum(w, axis=-1, keepdims=True)


def _moe(h, router_w, router_b, w_gate, w_up, w_down):
    B, S, D = h.shape
    T = B * S
    x = h.reshape(T, D)
    experts, gates = _route(x, router_w, router_b)
    A = T * TOP_K
    e_flat = experts.reshape(A)
    tok = jnp.arange(A, dtype=jnp.int32) // TOP_K
    order = jnp.argsort(e_flat)
    e_sorted = e_flat[order]
    tok_sorted = tok[order]
    g_sorted = gates.reshape(A)[order]
    counts = jnp.zeros((N_EXPERTS,), jnp.int32).at[e_flat].add(1)
    padded = (counts + MOE_BLOCK - 1) // MOE_BLOCK * MOE_BLOCK
    starts = jnp.cumsum(counts) - counts
    pad_ends = jnp.cumsum(padded)
    pad_starts = pad_ends - padded
    dest = pad_starts[e_sorted] + jnp.arange(A, dtype=jnp.int32) - starts[e_sorted]
    n_blocks = (A + N_EXPERTS * (MOE_BLOCK - 1) + MOE_BLOCK - 1) // MOE_BLOCK
    P = n_blocks * MOE_BLOCK
    buf_tok = jnp.full((P,), T, jnp.int32).at[dest].set(tok_sorted)
    buf_gate = jnp.zeros((P,), jnp.float32).at[dest].set(g_sorted)
    block_e = jnp.minimum(jnp.searchsorted(pad_ends, jnp.arange(n_blocks) * MOE_BLOCK, side='right'),
                          N_EXPERTS - 1)
    x_pad = jnp.concatenate([x, jnp.zeros((1, D), x.dtype)], axis=0)
    xb = x_pad[buf_tok].reshape(n_blocks, MOE_BLOCK, D)

    def expert_block(args):
        xe, e = args
        hid = jax.nn.silu(xe @ w_gate[e]) * (xe @ w_up[e])
        return hid @ w_down[e]

    yb = lax.map(expert_block, (xb, block_e)).reshape(P, D)
    y = jnp.zeros((T + 1, D), yb.dtype).at[buf_tok].add(yb * buf_gate[:, None].astype(yb.dtype))[:T]
    return y.reshape(B, S, D)


def _layer(x, mod, norm1_g, norm2_g, w_in, gmlp_g, w_s, b_s, w_proj_a, w_proj_b, w_out,
           router_w, router_b, w_gate, w_up, w_down, attend):
    shift1, scale1, gate1, shift2, scale2, gate2 = (mod[:, None, i] for i in range(N_MOD))
    h = _rmsnorm(x, norm1_g) * (1 + scale1) + shift1
    mix, k, v = _token_mixing(h, w_in, gmlp_g, w_s, b_s, w_proj_a, w_proj_b, w_out, attend)
    x = x + gate1 * mix
    h = _rmsnorm(x, norm2_g) * (1 + scale2) + shift2
    x = x + gate2 * _moe(h, router_w, router_b, w_gate, w_up, w_down)
    return x, k, v


def setup_inputs(seed: int = 0) -> dict:
    key = jax.random.key(seed)
    ks = jax.random.split(key, 24)
    f32 = jnp.float32

    def nrm(k, shape, scale):
        return jax.random.normal(k, shape, f32) * scale

    cache_shape = (DEC_BATCH, DEPTH, PAST_LEN, NA_HEADS, NA_HEAD_DIM)
    return {
        "x_prompt": nrm(ks[0], (BATCH, SEQ, D_MODEL), 1.0),
        "x_sample": nrm(ks[1], (DEC_BATCH, DEC_SEQ, D_MODEL), 1.0),
        "cache_k": nrm(ks[2], cache_shape, 1.0),
        "cache_v": nrm(ks[3], cache_shape, 1.0),
        "c": nrm(ks[4], (DEC_BATCH, D_MODEL), 1.0),
        "c_ctx": nrm(ks[5], (D_MODEL,), 1.0),
        "w_ada": nrm(ks[6], (DEPTH, D_MODEL, N_MOD * D_MODEL), 0.5 * D_MODEL ** -0.5),
        "b_ada": nrm(ks[7], (DEPTH, N_MOD * D_MODEL), 0.02),
        "norm1_g": 1.0 + nrm(ks[8], (DEPTH, D_MODEL), 0.1),
        "w_in": nrm(ks[9], (DEPTH, D_MODEL, IN_WIDTH), D_MODEL ** -0.5),
        "gmlp_g": 1.0 + nrm(ks[10], (DEPTH, GMLP_WIDTH), 0.1),
        "w_s": nrm(ks[11], (DEPTH, GMLP_GROUPS, GMLP_CHUNK, GMLP_CHUNK), GMLP_CHUNK ** -0.5),
        "b_s": 1.0 + nrm(ks[12], (DEPTH, GMLP_GROUPS, GMLP_CHUNK), 0.1),
        "rpb": nrm(ks[13], (DEPTH, NA_HEADS, 2 * NA_WIN_ROWS - 1, 2 * NA_WIN_COLS - 1), 0.2),
        "w_proj_a": nrm(ks[14], (DEPTH, GMLP_WIDTH, D_MODEL), GMLP_WIDTH ** -0.5),
        "w_proj_b": nrm(ks[15], (DEPTH, NA_WIDTH, D_MODEL), NA_WIDTH ** -0.5),
        "w_out": nrm(ks[16], (DEPTH, D_MODEL, D_MODEL), D_MODEL ** -0.5),
        "norm2_g": 1.0 + nrm(ks[17], (DEPTH, D_MODEL), 0.1),
        "router_w": nrm(ks[18], (D_MODEL, N_EXPERTS), D_MODEL ** -0.5),
        "router_b": nrm(ks[19], (N_EXPERTS,), 0.01),
        "w_gate": nrm(ks[20], (DEPTH, N_EXPERTS, D_MODEL, D_EXPERT), D_MODEL ** -0.5),
        "w_up": nrm(ks[21], (DEPTH, N_EXPERTS, D_MODEL, D_EXPERT), D_MODEL ** -0.5),
        "w_down": nrm(ks[22], (DEPTH, N_EXPERTS, D_EXPERT, D_MODEL), D_EXPERT ** -0.5),
        "final_g": 1.0 + nrm(ks[23], (D_MODEL,), 0.1),
    }


def reference(x_prompt, x_sample, cache_k, cache_v, c, c_ctx, w_ada, b_ada, norm1_g, w_in,
              gmlp_g, w_s, b_s, rpb, w_proj_a, w_proj_b, w_out, norm2_g, router_w, router_b,
              w_gate, w_up, w_down, final_g):
    xc = x_prompt
    xl = x_sample
    new_k_list = []
    new_v_list = []
    for l in range(DEPTH):
        mod_ctx = _modulation(c_ctx[None, :], w_ada[l], b_ada[l])
        xc, kc, vc = _layer(xc, mod_ctx, norm1_g[l], norm2_g[l], w_in[l], gmlp_g[l], w_s[l], b_s[l],
                            w_proj_a[l], w_proj_b[l], w_out[l], router_w, router_b,
                            w_gate[l], w_up[l], w_down[l], _context_attention)
        new_k_list.append(kc)
        new_v_list.append(vc)
        mod_lat = _modulation(c, w_ada[l], b_ada[l])
        attend_lat = functools.partial(_neighbourhood_attention, k_ctx=cache_k[:, l],
                                       v_ctx=cache_v[:, l], rpb=rpb[l])
        xl, _, _ = _layer(xl, mod_lat, norm1_g[l], norm2_g[l], w_in[l], gmlp_g[l], w_s[l], b_s[l],
                          w_proj_a[l], w_proj_b[l], w_out[l], router_w, router_b,
                          w_gate[l], w_up[l], w_down[l], attend_lat)
    y_prompt = _rmsnorm(xc, final_g)
    y_sample = _rmsnorm(xl, final_g)
    new_k = jnp.stack(new_k_list, axis=1)
    new_v = jnp.stack(new_v_list, axis=1)
    return (y_prompt, y_sample, new_k, new_v)
```

```python
import functools

import jax
import jax.numpy as jnp
from jax import lax
from jax.experimental import pallas as pl
from jax.experimental.pallas import tpu as pltpu

F32 = jnp.float32
BF16 = jnp.bfloat16
I32 = jnp.int32

EPS = 1e-6
N_MOD = 6
GRID_W = 64
WIN_ROWS = 8
WIN_COLS = 16
GMLP_CHUNK = 128
GMLP_GROUPS = 8
N_EXPERTS = 16
N_EXPERT_GROUPS = 4
EXPERTS_PER_GROUP = N_EXPERTS // N_EXPERT_GROUPS
TOP_K = 2
MOE_BLOCK = 256
COND_ROWS = 8
ROW_CHUNK = 64
NA_ROWS_PER_STEP = 8
NA_HEADS_PER_STEP = 4
MIB = 1024 * 1024

_NT = (((1,), (1,)), ((), ()))


def _tile(n, pref):
    return pref if n % pref == 0 else n


def _params(n_axes, vmem_mib):
    return pltpu.CompilerParams(dimension_semantics=("arbitrary",) * n_axes,
                                vmem_limit_bytes=vmem_mib * MIB)


def _for_chunks(n, c, fn):
    c = min(c, n)

    def body(t, carry):
        fn(pl.ds(pl.multiple_of(t * c, c), c))
        return carry

    lax.fori_loop(0, n // c, body, 0)


def _sigmoid(x):
    return 1.0 / (1.0 + jnp.exp(-x))


def _gelu_tanh(x):
    return x * (0.5 * (1.0 + jnp.tanh(0.7978845608028654 * (x + 0.044715 * (x * x * x)))))


def _rms_scale(x):
    return x * lax.rsqrt(jnp.mean(x * x, axis=-1, keepdims=True) + EPS)


def _ada_kernel(c_ref, w_ref, b_ref, o_ref):
    c = c_ref[...]
    s = (c * _sigmoid(c)).astype(BF16)
    o_ref[0] = jnp.dot(s, w_ref[0].astype(BF16), preferred_element_type=F32) + b_ref[0]


def _modulation_all(cond, w_ada, b_ada):
    L, D, N = w_ada.shape
    tn = _tile(N, 1024)
    out = pl.pallas_call(
        _ada_kernel,
        grid=(L, N // tn),
        in_specs=[pl.BlockSpec((COND_ROWS, D), lambda l, j: (0, 0)),
                  pl.BlockSpec((1, D, tn), lambda l, j: (l, 0, j)),
                  pl.BlockSpec((1, 1, tn), lambda l, j: (l, 0, j))],
        out_specs=pl.BlockSpec((1, COND_ROWS, tn), lambda l, j: (l, 0, j)),
        out_shape=jax.ShapeDtypeStruct((L, COND_ROWS, N), F32),
        compiler_params=_params(2, 40),
        name="ada_modulation",
    )(cond, w_ada, b_ada.reshape(L, 1, N))
    return out.reshape(L, COND_ROWS, N_MOD, D)


def _proj_kernel(x_ref, mod_ref, g_ref, w_ref, o_ref, k_ref, v_ref, h_scr, *, ncb, jk0, jv0, nkv):
    i = pl.program_id(0)
    j = pl.program_id(1)

    @pl.when(j == 0)
    def _():
        g = g_ref[0]
        scale1 = 1.0 + mod_ref[0, 0, 1:2, :]
        shift1 = mod_ref[0, 0, 0:1, :]

        def chunk(rows):
            h_scr[rows, :] = (_rms_scale(x_ref[rows, :]) * g * scale1 + shift1).astype(BF16)

        _for_chunks(x_ref.shape[0], ROW_CHUNK, chunk)

    acc = jnp.dot(h_scr[...], w_ref[0], preferred_element_type=F32)
    o_ref[...] = acc.astype(BF16)

    @pl.when((i < ncb) & (j >= jk0) & (j < jk0 + nkv))
    def _():
        k_ref[...] = acc

    @pl.when((i < ncb) & (j >= jv0) & (j < jv0 + nkv))
    def _():
        v_ref[...] = acc


def _input_projection(x, mod, norm_g, w_in, l, nc, cond_of_block, tm):
    T, D = x.shape
    N = w_in.shape[-1]
    tn = _tile(D, 512)
    ncb = nc // tm
    jk0, jv0, nkv = D // tn, 2 * D // tn, D // tn
    kernel = functools.partial(_proj_kernel, ncb=ncb, jk0=jk0, jv0=jv0, nkv=nkv)

    def kv_block(j0):
        def index(i, j):
            col = jnp.where(i < ncb, jnp.clip(j - j0, 0, nkv - 1), nkv - 1)
            return (jnp.minimum(i, ncb - 1), col)
        return pl.BlockSpec((tm, tn), index)

    return pl.pallas_call(
        kernel,
        grid=(T // tm, N // tn),
        in_specs=[pl.BlockSpec((tm, D), lambda i, j: (i, 0)),
                  pl.BlockSpec((1, 1, N_MOD, D), lambda i, j: (l, cond_of_block(i), 0, 0)),
                  pl.BlockSpec((1, 1, D), lambda i, j: (l, 0, 0)),
                  pl.BlockSpec((1, D, tn), lambda i, j: (l, 0, j))],
        out_specs=[pl.BlockSpec((tm, tn), lambda i, j: (i, j)), kv_block(jk0), kv_block(jv0)],
        out_shape=[jax.ShapeDtypeStruct((T, N), BF16),
                   jax.ShapeDtypeStruct((nc, D), F32),
                   jax.ShapeDtypeStruct((nc, D), F32)],
        scratch_shapes=[pltpu.VMEM((tm, D), BF16)],
        compiler_params=_params(2, 48),
        name="norm1_input_projection",
    )(x, mod, norm_g.reshape(-1, 1, D), w_in)


def _gmlp_kernel(u_ref, gv_ref, g_ref, ws_ref, bs_ref, o_ref):
    gw = u_ref.shape[1] // GMLP_GROUPS
    g = g_ref[0]

    def chunk(rows):
        v = _rms_scale(_gelu_tanh(gv_ref[rows, :].astype(F32))) * g
        v = v.astype(BF16)
        for grp in range(GMLP_GROUPS):
            cols = slice(grp * gw, (grp + 1) * gw)
            s = jnp.dot(ws_ref[grp], v[:, cols], preferred_element_type=F32) + bs_ref[grp]
            u = _gelu_tanh(u_ref[rows, cols].astype(F32))
            o_ref[rows, cols] = (u * s).astype(BF16)

    _for_chunks(u_ref.shape[0], GMLP_CHUNK, chunk)


def _gmlp(proj, gmlp_g, w_s, b_s, l, D, tm):
    T = proj.shape[0]
    gw = D // GMLP_GROUPS
    bs = jnp.broadcast_to(b_s[l][:, :, None], (GMLP_GROUPS, GMLP_CHUNK, gw))
    return pl.pallas_call(
        _gmlp_kernel,
        grid=(T // tm,),
        in_specs=[pl.BlockSpec((tm, D), lambda i: (i, 3)),
                  pl.BlockSpec((tm, D), lambda i: (i, 4)),
                  pl.BlockSpec((1, 1, D), lambda i: (l, 0, 0)),
                  pl.BlockSpec((None, GMLP_GROUPS, GMLP_CHUNK, GMLP_CHUNK), lambda i: (l, 0, 0, 0)),
                  pl.BlockSpec((GMLP_GROUPS, GMLP_CHUNK, gw), lambda i: (0, 0, 0))],
        out_specs=pl.BlockSpec((tm, D), lambda i: (i, 0)),
        out_shape=jax.ShapeDtypeStruct((T, D), BF16),
        compiler_params=_params(1, 40),
        name="gmlp_spatial_gating",
    )(proj, proj, gmlp_g.reshape(-1, 1, D), w_s, bs)


def _ctx_attn_kernel(q_ref, k_ref, v_ref, o_ref, *, heads, dh):
    scale = dh ** -0.5
    for h in range(heads):
        cols = slice(h * dh, (h + 1) * dh)
        s = lax.dot_general(q_ref[:, cols], k_ref[:, cols], _NT, preferred_element_type=F32) * scale
        p = jnp.exp(s - jnp.max(s, axis=-1, keepdims=True))
        denom = jnp.sum(p, axis=-1, keepdims=True)
        o = jnp.dot(p.astype(BF16), v_ref[:, cols], preferred_element_type=F32)
        o_ref[:, cols] = (o / denom).astype(BF16)


def _context_attention(proj, batch, seq, D, heads, dh):
    T = proj.shape[0]
    return pl.pallas_call(
        functools.partial(_ctx_attn_kernel, heads=heads, dh=dh),
        grid=(batch,),
        in_specs=[pl.BlockSpec((seq, D), lambda b: (b, 0)),
                  pl.BlockSpec((seq, D), lambda b: (b, 1)),
                  pl.BlockSpec((seq, D), lambda b: (b, 2))],
        out_specs=pl.BlockSpec((seq, D), lambda b: (b, 0)),
        out_shape=jax.ShapeDtypeStruct((T, D), BF16),
        compiler_params=_params(1, 32),
        name="context_attention",
    )(proj, proj, proj)


def _na_kernel(q_ref, k_ref, v_ref, kc_ref, vc_ref, bias_ref, yb_ref, o_ref, *, hps, dh, rows, kh, rps):
    del yb_ref
    scale = dh ** -0.5
    rb = pl.program_id(2)
    band = kh * GRID_W

    def row_body(ri, carry):
        r = rb * rps + ri
        rs = jnp.clip(r - kh // 2, 0, rows - kh)
        variant = rs - r + (WIN_ROWS - 1) - (WIN_ROWS - kh)
        qrows = pl.ds(pl.multiple_of(ri * GRID_W, GRID_W), GRID_W)
        krows = pl.ds(pl.multiple_of(rs * GRID_W, GRID_W), band)
        for h in range(hps):
            cols = slice(h * dh, (h + 1) * dh)
            q = q_ref[qrows, cols]
            s_loc = lax.dot_general(q, k_ref[krows, cols], _NT, preferred_element_type=F32) * scale
            s_loc = s_loc + bias_ref[variant, h]
            s_ctx = lax.dot_general(q, kc_ref[0, :, cols], _NT, preferred_element_type=F32) * scale
            m = jnp.maximum(jnp.max(s_loc, axis=-1, keepdims=True), jnp.max(s_ctx, axis=-1, keepdims=True))
            p_loc = jnp.exp(s_loc - m)
            p_ctx = jnp.exp(s_ctx - m)
            denom = jnp.sum(p_loc, axis=-1, keepdims=True) + jnp.sum(p_ctx, axis=-1, keepdims=True)
            o = (jnp.dot(p_loc.astype(BF16), v_ref[krows, cols], preferred_element_type=F32)
                 + jnp.dot(p_ctx.astype(BF16), vc_ref[0, :, cols], preferred_element_type=F32))
            o_ref[qrows, cols] = (o / denom).astype(BF16)
        return carry

    lax.fori_loop(0, rps, row_body, 0)


def _na_bias_table(rpb_l, rows, kh):
    col = jnp.arange(GRID_W)
    cs = jnp.clip(col - WIN_COLS // 2, 0, GRID_W - WIN_COLS)
    col_valid = (col[None, :] >= cs[:, None]) & (col[None, :] < cs[:, None] + WIN_COLS)
    col_idx = jnp.clip(col[None, :] - col[:, None] + WIN_COLS - 1, 0, 2 * WIN_COLS - 2)
    n_var = kh
    dr = jnp.arange(n_var)[:, None] + jnp.arange(kh)[None, :] + (WIN_ROWS - kh)
    b = rpb_l[:, dr][:, :, :, col_idx]
    b = jnp.where(col_valid[None, None, None], b, -jnp.inf)
    b = b.transpose(1, 0, 3, 2, 4)
    return b.reshape(n_var, rpb_l.shape[0], GRID_W, kh * GRID_W).astype(F32)


def _neighbourhood_attention(proj, yb, kc, vc, bias, nc, dec_batch, dec_seq, D, heads, dh):
    rows = dec_seq // GRID_W
    kh = min(WIN_ROWS, rows)
    rps = min(NA_ROWS_PER_STEP, rows)
    hps = min(NA_HEADS_PER_STEP, heads)
    qn = rps * GRID_W
    cw = hps * dh
    n_hg = heads // hps
    q0 = nc // qn
    k0 = nc // dec_seq
    ncols = D // cw
    kernel = functools.partial(_na_kernel, hps=hps, dh=dh, rows=rows, kh=kh, rps=rps)
    return pl.pallas_call(
        kernel,
        grid=(dec_batch, n_hg, rows // rps),
        in_specs=[pl.BlockSpec((qn, cw), lambda b, g, r: (q0 + b * (rows // rps) + r, g)),
                  pl.BlockSpec((dec_seq, cw), lambda b, g, r: (k0 + b, ncols + g)),
                  pl.BlockSpec((dec_seq, cw), lambda b, g, r: (k0 + b, 2 * ncols + g)),
                  pl.BlockSpec((1, kc.shape[1], cw), lambda b, g, r: (b, 0, g)),
                  pl.BlockSpec((1, vc.shape[1], cw), lambda b, g, r: (b, 0, g)),
                  pl.BlockSpec((bias.shape[0], hps, GRID_W, kh * GRID_W), lambda b, g, r: (0, g, 0, 0)),
                  pl.BlockSpec(memory_space=pl.ANY)],
        out_specs=pl.BlockSpec((qn, cw), lambda b, g, r: (q0 + b * (rows // rps) + r, g)),
        out_shape=jax.ShapeDtypeStruct(yb.shape, yb.dtype),
        input_output_aliases={6: 0},
        compiler_params=_params(3, 48),
        name="neighbourhood_attention",
    )(proj, proj, proj, kc, vc, bias, yb)


def _merge_kernel(ya_ref, yb_ref, ga_ref, gb_ref, wa_ref, wb_ref, o_ref):
    a = jnp.dot(ya_ref[...], wa_ref[0], preferred_element_type=F32)
    b = jnp.dot(yb_ref[...], wb_ref[0], preferred_element_type=F32)
    o = _sigmoid(ga_ref[...].astype(F32)) * a + _sigmoid(gb_ref[...].astype(F32)) * b
    o_ref[...] = o.astype(BF16)


def _merge(ya, yb, proj, w_pa, w_pb, l, D, tm):
    T = ya.shape[0]
    tn = _tile(D, 512)
    nj = D // tn
    return pl.pallas_call(
        _merge_kernel,
        grid=(T // tm, nj),
        in_specs=[pl.BlockSpec((tm, D), lambda i, j: (i, 0)),
                  pl.BlockSpec((tm, D), lambda i, j: (i, 0)),
                  pl.BlockSpec((tm, tn), lambda i, j: (i, 5 * nj + j)),
                  pl.BlockSpec((tm, tn), lambda i, j: (i, 6 * nj + j)),
                  pl.BlockSpec((1, D, tn), lambda i, j: (l, 0, j)),
                  pl.BlockSpec((1, D, tn), lambda i, j: (l, 0, j))],
        out_specs=pl.BlockSpec((tm, tn), lambda i, j: (i, j)),
        out_shape=jax.ShapeDtypeStruct((T, D), BF16),
        compiler_params=_params(2, 48),
        name="gated_branch_merge",
    )(ya, yb, proj, proj, w_pa, w_pb)


def _argmax_first(vals):
    best = vals[0]
    idx = jnp.zeros(best.shape, I32)
    for k in range(1, len(vals)):
        upd = vals[k] > best
        idx = jnp.where(upd, k, idx)
        best = jnp.where(upd, vals[k], best)
    return idx


def _select(idx, vals):
    out = vals[-1]
    for k in range(len(vals) - 2, -1, -1):
        out = jnp.where(idx == k, vals[k], out)
    return out


def _route(logits_t, router_b):
    m = jnp.max(logits_t, axis=0, keepdims=True)
    ex = jnp.exp(logits_t - m)
    probs = ex / jnp.sum(ex, axis=0, keepdims=True)
    sel = probs + router_b
    sel_rows = [sel[e:e + 1, :] for e in range(N_EXPERTS)]
    prob_rows = [probs[e:e + 1, :] for e in range(N_EXPERTS)]
    scores = []
    for g in range(N_EXPERT_GROUPS):
        a, b, c, d = sel_rows[g * EXPERTS_PER_GROUP:(g + 1) * EXPERTS_PER_GROUP]
        hi1, lo1 = jnp.maximum(a, b), jnp.minimum(a, b)
        hi2, lo2 = jnp.maximum(c, d), jnp.minimum(c, d)
        top1 = jnp.maximum(hi1, hi2)
        top2 = jnp.maximum(jnp.minimum(hi1, hi2), jnp.where(hi1 >= hi2, lo1, lo2))
        scores.append(top1 + top2)
    best_g = _argmax_first(scores)
    in_sel = [_select(best_g, [sel_rows[g * EXPERTS_PER_GROUP + k] for g in range(N_EXPERT_GROUPS)])
              for k in range(EXPERTS_PER_GROUP)]
    in_prob = [_select(best_g, [prob_rows[g * EXPERTS_PER_GROUP + k] for g in range(N_EXPERT_GROUPS)])
               for k in range(EXPERTS_PER_GROUP)]
    i1 = _argmax_first(in_sel)
    i2 = _argmax_first([jnp.where(i1 == k, -jnp.inf, in_sel[k]) for k in range(EXPERTS_PER_GROUP)])
    w1 = _select(i1, in_prob)
    w2 = _select(i2, in_prob)
    tot = w1 + w2
    base = best_g * EXPERTS_PER_GROUP
    return (base + i1, base + i2), (w1 / tot, w2 / tot)


def _out_kernel(mg_ref, x_ref, mod_ref, g_ref, w_ref, rw_ref, rb_ref, xo_ref, h_ref, e_ref, gt_ref, acc_scr, hf_scr):
    acc_scr[...] = jnp.dot(mg_ref[...], w_ref[0], preferred_element_type=F32)
    gate1 = mod_ref[0, 0, 2:3, :]
    shift2 = mod_ref[0, 0, 3:4, :]
    scale2 = 1.0 + mod_ref[0, 0, 4:5, :]
    g = g_ref[0]

    def chunk(rows):
        xn = x_ref[rows, :] + gate1 * acc_scr[rows, :]
        xo_ref[rows, :] = xn
        h = _rms_scale(xn) * g * scale2 + shift2
        hf_scr[rows, :] = h
        h_ref[rows, :] = h.astype(BF16)

    _for_chunks(x_ref.shape[0], ROW_CHUNK, chunk)
    logits_t = lax.dot_general(rw_ref[...], hf_scr[...], _NT, preferred_element_type=F32,
                               precision=lax.Precision.HIGHEST)
    experts, gates = _route(logits_t, rb_ref[...])
    for k in range(TOP_K):
        e_ref[k:k + 1, :] = experts[k]
        gt_ref[k:k + 1, :] = gates[k]


def _output_projection(merged, x, mod, norm_g, w_out, router_wt, router_b, l, cond_of_block, tm):
    T, D = x.shape
    E = router_wt.shape[0]
    return pl.pallas_call(
        _out_kernel,
        grid=(T // tm,),
        in_specs=[pl.BlockSpec((tm, D), lambda i: (i, 0)),
                  pl.BlockSpec((tm, D), lambda i: (i, 0)),
                  pl.BlockSpec((1, 1, N_MOD, D), lambda i: (l, cond_of_block(i), 0, 0)),
                  pl.BlockSpec((1, 1, D), lambda i: (l, 0, 0)),
                  pl.BlockSpec((1, D, D), lambda i: (l, 0, 0)),
                  pl.BlockSpec((E, D), lambda i: (0, 0)),
                  pl.BlockSpec((E, 1), lambda i: (0, 0))],
        out_specs=[pl.BlockSpec((tm, D), lambda i: (i, 0)),
                   pl.BlockSpec((tm, D), lambda i: (i, 0)),
                   pl.BlockSpec((TOP_K, tm), lambda i: (0, i)),
                   pl.BlockSpec((TOP_K, tm), lambda i: (0, i))],
        out_shape=[jax.ShapeDtypeStruct((T, D), F32),
                   jax.ShapeDtypeStruct((T, D), BF16),
                   jax.ShapeDtypeStruct((TOP_K, T), I32),
                   jax.ShapeDtypeStruct((TOP_K, T), F32)],
        scratch_shapes=[pltpu.VMEM((tm, D), F32), pltpu.VMEM((tm, D), F32)],
        compiler_params=_params(1, 56),
        name="output_projection_norm2_router",
    )(merged, x, mod, norm_g.reshape(-1, 1, D), w_out, router_wt, router_b.reshape(E, 1))


def _moe_kernel(be_ref, nu_ref, x_ref, wg_ref, wu_ref, wd_ref, o_ref):
    del be_ref
    b = pl.program_id(0)

    @pl.when(b < nu_ref[0])
    def _():
        x = x_ref[...]
        g = jnp.dot(x, wg_ref[0], preferred_element_type=F32)
        u = jnp.dot(x, wu_ref[0], preferred_element_type=F32)
        hid = (g * _sigmoid(g)) * u
        o_ref[...] = jnp.dot(hid.astype(BF16), wd_ref[0], preferred_element_type=F32)

    @pl.when(b >= nu_ref[0])
    def _():
        o_ref[...] = jnp.zeros(o_ref.shape, o_ref.dtype)


def _moe_experts(xb, block_e, n_used, w_gate, w_up, w_down, l):
    P, D = xb.shape
    E, DE = w_gate.shape[1], w_gate.shape[3]
    nb = P // MOE_BLOCK
    w_gate = w_gate.reshape(-1, D, DE)
    w_up = w_up.reshape(-1, D, DE)
    w_down = w_down.reshape(-1, DE, D)
    grid_spec = pltpu.PrefetchScalarGridSpec(
        num_scalar_prefetch=2,
        grid=(nb,),
        in_specs=[pl.BlockSpec((MOE_BLOCK, D), lambda b, be, nu: (b, 0)),
                  pl.BlockSpec((1, D, DE), lambda b, be, nu: (l * E + be[b], 0, 0)),
                  pl.BlockSpec((1, D, DE), lambda b, be, nu: (l * E + be[b], 0, 0)),
                  pl.BlockSpec((1, DE, D), lambda b, be, nu: (l * E + be[b], 0, 0))],
        out_specs=pl.BlockSpec((MOE_BLOCK, D), lambda b, be, nu: (b, 0)),
    )
    return pl.pallas_call(
        _moe_kernel,
        grid_spec=grid_spec,
        out_shape=jax.ShapeDtypeStruct((P, D), F32),
        compiler_params=_params(1, 48),
        name="moe_experts",
    )(block_e, n_used, xb, w_gate, w_up, w_down)


def _dispatch_plan(experts, T):
    A = TOP_K * T
    e_flat = experts.reshape(A)
    tok = jnp.arange(A, dtype=I32) % T
    order = jnp.argsort(e_flat)
    e_sorted = e_flat[order]
    counts = jnp.sum((e_flat[:, None] == jnp.arange(N_EXPERTS, dtype=I32)[None, :]).astype(I32), axis=0)
    padded = (counts + MOE_BLOCK - 1) // MOE_BLOCK * MOE_BLOCK
    starts = jnp.cumsum(counts) - counts
    pad_ends = jnp.cumsum(padded)
    pad_starts = pad_ends - padded
    dest = pad_starts[e_sorted] + jnp.arange(A, dtype=I32) - starts[e_sorted]
    n_blocks = (A + N_EXPERTS * (MOE_BLOCK - 1) + MOE_BLOCK - 1) // MOE_BLOCK
    P = n_blocks * MOE_BLOCK
    buf_tok = jnp.zeros((P,), I32).at[dest].set(tok[order])
    pos = jnp.zeros((A,), I32).at[order].set(dest.astype(I32))
    block_e = jnp.minimum(jnp.searchsorted(pad_ends, jnp.arange(n_blocks, dtype=I32) * MOE_BLOCK, side='right'),
                          N_EXPERTS - 1).astype(I32)
    n_used = (pad_ends[-1] // MOE_BLOCK).astype(I32).reshape(1)
    return buf_tok, pos, block_e, n_used


def _combine_kernel(x_ref, y1_ref, y2_ref, g1_ref, g2_ref, mod_ref, o_ref):
    gate2 = mod_ref[0, 0, 5:6, :]

    def chunk(rows):
        y = g1_ref[rows, :] * y1_ref[rows, :] + g2_ref[rows, :] * y2_ref[rows, :]
        o_ref[rows, :] = x_ref[rows, :] + gate2 * y

    _for_chunks(x_ref.shape[0], ROW_CHUNK, chunk)


def _combine(x, y1, y2, g1, g2, mod, l, cond_of_block, tm):
    T, D = x.shape
    row = pl.BlockSpec((tm, D), lambda i: (i, 0))
    col = pl.BlockSpec((tm, 1), lambda i: (i, 0))
    return pl.pallas_call(
        _combine_kernel,
        grid=(T // tm,),
        in_specs=[row, row, row, col, col,
                  pl.BlockSpec((1, 1, N_MOD, D), lambda i: (l, cond_of_block(i), 0, 0))],
        out_specs=row,
        out_shape=jax.ShapeDtypeStruct((T, D), F32),
        compiler_params=_params(1, 48),
        name="moe_combine_residual",
    )(x, y1, y2, g1, g2, mod)


def _final_norm_kernel(x_ref, g_ref, o_ref):
    g = g_ref[...]

    def chunk(rows):
        o_ref[rows, :] = _rms_scale(x_ref[rows, :]) * g

    _for_chunks(x_ref.shape[0], ROW_CHUNK, chunk)


def _final_norm(x, g, row0, n_rows, tm):
    D = x.shape[1]
    b0 = row0 // tm
    return pl.pallas_call(
        _final_norm_kernel,
        grid=(n_rows // tm,),
        in_specs=[pl.BlockSpec((tm, D), lambda i: (b0 + i, 0)),
                  pl.BlockSpec((1, D), lambda i: (0, 0))],
        out_specs=pl.BlockSpec((tm, D), lambda i: (i, 0)),
        out_shape=jax.ShapeDtypeStruct((n_rows, D), F32),
        compiler_params=_params(1, 32),
        name="final_norm",
    )(x, g.reshape(1, D))


def kernel(x_prompt, x_sample, cache_k, cache_v, c, c_ctx, w_ada, b_ada, norm1_g, w_in, gmlp_g, w_s, b_s, rpb,
           w_proj_a, w_proj_b, w_out, norm2_g, router_w, router_b, w_gate, w_up, w_down, final_g):
    batch, seq, D = x_prompt.shape
    dec_batch, dec_seq, _ = x_sample.shape
    depth = w_in.shape[0]
    heads, dh = cache_k.shape[3], cache_k.shape[4]
    past = cache_k.shape[2]
    nc, nl = batch * seq, dec_batch * dec_seq
    T = nc + nl
    assert heads * dh == D and gmlp_g.shape[1] == D and w_in.shape[2] == 7 * D
    assert 1 + dec_batch <= COND_ROWS
    assert router_w.shape[1] == N_EXPERTS
    assert seq % GMLP_CHUNK == 0 and dec_seq % GMLP_CHUNK == 0 and dec_seq % GRID_W == 0
    assert nc % dec_seq == 0

    tm_big = 1024 if (nc % 1024 == 0 and dec_seq % 1024 == 0) else GMLP_CHUNK
    tm_mid = 512 if (nc % 512 == 0 and dec_seq % 512 == 0) else GMLP_CHUNK

    def cond_of(tm):
        ncb, per = nc // tm, dec_seq // tm
        return lambda i: jnp.where(i < ncb, 0, 1 + (i - ncb) // per)

    cond = jnp.zeros((COND_ROWS, D), F32).at[0].set(c_ctx).at[1:1 + dec_batch].set(c)
    mod = _modulation_all(cond, w_ada, b_ada)

    w_in_b = w_in.astype(BF16)
    w_s_b = w_s.astype(BF16)
    w_pa_b = w_proj_a.astype(BF16)
    w_pb_b = w_proj_b.astype(BF16)
    w_out_b = w_out.astype(BF16)
    w_gate_b = w_gate.astype(BF16)
    w_up_b = w_up.astype(BF16)
    w_down_b = w_down.astype(BF16)
    kc_all = cache_k.astype(BF16).reshape(dec_batch, depth, past, D)
    vc_all = cache_v.astype(BF16).reshape(dec_batch, depth, past, D)
    router_wt = router_w.T
    rows = dec_seq // GRID_W
    kh = min(WIN_ROWS, rows)

    x = jnp.concatenate([x_prompt.reshape(nc, D), x_sample.reshape(nl, D)], axis=0)
    new_k, new_v = [], []
    for l in range(depth):
        proj, k_ctx, v_ctx = _input_projection(x, mod, norm1_g, w_in_b, l, nc, cond_of(tm_big), tm_big)
        new_k.append(k_ctx.reshape(batch, seq, heads, dh))
        new_v.append(v_ctx.reshape(batch, seq, heads, dh))
        ya = _gmlp(proj, gmlp_g, w_s_b, b_s, l, D, tm_mid)
        yb = _context_attention(proj, batch, seq, D, heads, dh)
        bias = _na_bias_table(rpb[l], rows, kh)
        yb = _neighbourhood_attention(proj, yb, kc_all[:, l], vc_all[:, l], bias, nc, dec_batch, dec_seq, D, heads, dh)
        merged = _merge(ya, yb, proj, w_pa_b, w_pb_b, l, D, tm_big)
        x, h2, experts, gates = _output_projection(merged, x, mod, norm2_g, w_out_b, router_wt, router_b, l,
                                                   cond_of(tm_mid), tm_mid)
        buf_tok, pos, block_e, n_used = _dispatch_plan(experts, T)
        yexp = _moe_experts(h2[buf_tok], block_e, n_used, w_gate_b, w_up_b, w_down_b, l)
        x = _combine(x, yexp[pos[:T]], yexp[pos[T:]], gates[0][:, None], gates[1][:, None], mod, l,
                     cond_of(tm_mid), tm_mid)

    y_prompt = _final_norm(x, final_g, 0, nc, tm_mid).reshape(batch, seq, D)
    y_sample = _final_norm(x, final_g, nc, nl, tm_mid).reshape(dec_batch, dec_seq, D)
    return (y_prompt, y_sample, jnp.stack(new_k, axis=1), jnp.stack(new_v, axis=1))
```

```python
import functools

import jax
import jax.numpy as jnp
from jax import lax
from jax.experimental import pallas as pl
from jax.experimental.pallas import tpu as pltpu

F32 = jnp.float32
BF16 = jnp.bfloat16
I32 = jnp.int32

EPS = 1e-6
N_MOD = 6
GRID_W = 64
WIN_ROWS = 8
WIN_COLS = 16
GMLP_CHUNK = 128
GMLP_GROUPS = 8
N_EXPERTS = 16
N_EXPERT_GROUPS = 4
EXPERTS_PER_GROUP = N_EXPERTS // N_EXPERT_GROUPS
TOP_K = 2
MOE_BLOCK = 256
COND_ROWS = 8
ROW_CHUNK = 64
NA_QROWS = 4
NA_KROWS = 12
NA_KBLOCK = 256
NA_HEADS_PER_STEP = 4
MIB = 1024 * 1024

_NT = (((1,), (1,)), ((), ()))


def _tile(n, pref):
    return pref if n % pref == 0 else n


def _params(n_axes, vmem_mib):
    return pltpu.CompilerParams(dimension_semantics=("arbitrary",) * n_axes,
                                vmem_limit_bytes=vmem_mib * MIB)


def _for_chunks(n, c, fn):
    c = min(c, n)

    def body(t, carry):
        fn(pl.ds(pl.multiple_of(t * c, c), c))
        return carry

    lax.fori_loop(0, n // c, body, 0)


def _sigmoid(x):
    return 1.0 / (1.0 + jnp.exp(-x))


def _gelu_tanh(x):
    return x * (0.5 * (1.0 + jnp.tanh(0.7978845608028654 * (x + 0.044715 * (x * x * x)))))


def _rms_scale(x):
    return x * lax.rsqrt(jnp.mean(x * x, axis=-1, keepdims=True) + EPS)


def _ada_kernel(c_ref, w_ref, b_ref, o_ref):
    c = c_ref[...]
    s = (c * _sigmoid(c)).astype(BF16)
    o_ref[0] = jnp.dot(s, w_ref[0].astype(BF16), preferred_element_type=F32) + b_ref[0]


def _modulation_all(cond, w_ada, b_ada):
    L, D, N = w_ada.shape
    tn = _tile(N, 1024)
    out = pl.pallas_call(
        _ada_kernel,
        grid=(L, N // tn),
        in_specs=[pl.BlockSpec((COND_ROWS, D), lambda l, j: (0, 0)),
                  pl.BlockSpec((1, D, tn), lambda l, j: (l, 0, j)),
                  pl.BlockSpec((1, 1, tn), lambda l, j: (l, 0, j))],
        out_specs=pl.BlockSpec((1, COND_ROWS, tn), lambda l, j: (l, 0, j)),
        out_shape=jax.ShapeDtypeStruct((L, COND_ROWS, N), F32),
        compiler_params=_params(2, 40),
        name="ada_modulation",
    )(cond, w_ada, b_ada.reshape(L, 1, N))
    return out.reshape(L, COND_ROWS, N_MOD, D)


def _proj_kernel(x_ref, mod_ref, g_ref, w_ref, o_ref, k_ref, v_ref, h_scr, *, ncb, jk0, jv0, nkv, qscale):
    i = pl.program_id(0)
    j = pl.program_id(1)

    @pl.when(j == 0)
    def _():
        g = g_ref[0]
        scale1 = 1.0 + mod_ref[0, 0, 1:2, :]
        shift1 = mod_ref[0, 0, 0:1, :]

        def chunk(rows):
            h_scr[rows, :] = (_rms_scale(x_ref[rows, :]) * g * scale1 + shift1).astype(BF16)

        _for_chunks(x_ref.shape[0], ROW_CHUNK, chunk)

    acc = jnp.dot(h_scr[...], w_ref[0], preferred_element_type=F32)
    o_ref[...] = (acc * jnp.where(j < jk0, qscale, 1.0)).astype(BF16)

    @pl.when((i < ncb) & (j >= jk0) & (j < jk0 + nkv))
    def _():
        k_ref[...] = acc

    @pl.when((i < ncb) & (j >= jv0) & (j < jv0 + nkv))
    def _():
        v_ref[...] = acc


def _input_projection(x, mod, norm_g, w_in, l, nc, cond_of_block, tm, qscale):
    T, D = x.shape
    N = w_in.shape[-1]
    tn = _tile(D, 512)
    ncb = nc // tm
    jk0, jv0, nkv = D // tn, 2 * D // tn, D // tn
    kernel = functools.partial(_proj_kernel, ncb=ncb, jk0=jk0, jv0=jv0, nkv=nkv, qscale=qscale)

    def kv_block(j0):
        def index(i, j):
            col = jnp.where(i < ncb, jnp.clip(j - j0, 0, nkv - 1), nkv - 1)
            return (jnp.minimum(i, ncb - 1), col)
        return pl.BlockSpec((tm, tn), index)

    return pl.pallas_call(
        kernel,
        grid=(T // tm, N // tn),
        in_specs=[pl.BlockSpec((tm, D), lambda i, j: (i, 0)),
                  pl.BlockSpec((1, 1, N_MOD, D), lambda i, j: (l, cond_of_block(i), 0, 0)),
                  pl.BlockSpec((1, 1, D), lambda i, j: (l, 0, 0)),
                  pl.BlockSpec((1, D, tn), lambda i, j: (l, 0, j))],
        out_specs=[pl.BlockSpec((tm, tn), lambda i, j: (i, j)), kv_block(jk0), kv_block(jv0)],
        out_shape=[jax.ShapeDtypeStruct((T, N), BF16),
                   jax.ShapeDtypeStruct((nc, D), F32),
                   jax.ShapeDtypeStruct((nc, D), F32)],
        scratch_shapes=[pltpu.VMEM((tm, D), BF16)],
        compiler_params=_params(2, 48),
        name="norm1_input_projection",
    )(x, mod, norm_g.reshape(-1, 1, D), w_in)


def _gmlp_kernel(u_ref, gv_ref, g_ref, ws_ref, bs_ref, o_ref):
    gw = u_ref.shape[1] // GMLP_GROUPS
    g = g_ref[0]

    def chunk(rows):
        v = _rms_scale(_gelu_tanh(gv_ref[rows, :].astype(F32))) * g
        v = v.astype(BF16)
        for grp in range(GMLP_GROUPS):
            cols = slice(grp * gw, (grp + 1) * gw)
            s = jnp.dot(ws_ref[grp], v[:, cols], preferred_element_type=F32) + bs_ref[grp]
            u = _gelu_tanh(u_ref[rows, cols].astype(F32))
            o_ref[rows, cols] = (u * s).astype(BF16)

    _for_chunks(u_ref.shape[0], GMLP_CHUNK, chunk)


def _gmlp(proj, gmlp_g, w_s, b_s, l, D, tm):
    T = proj.shape[0]
    gw = D // GMLP_GROUPS
    bs = jnp.broadcast_to(b_s[l][:, :, None], (GMLP_GROUPS, GMLP_CHUNK, gw))
    return pl.pallas_call(
        _gmlp_kernel,
        grid=(T // tm,),
        in_specs=[pl.BlockSpec((tm, D), lambda i: (i, 3)),
                  pl.BlockSpec((tm, D), lambda i: (i, 4)),
                  pl.BlockSpec((1, 1, D), lambda i: (l, 0, 0)),
                  pl.BlockSpec((None, GMLP_GROUPS, GMLP_CHUNK, GMLP_CHUNK), lambda i: (l, 0, 0, 0)),
                  pl.BlockSpec((GMLP_GROUPS, GMLP_CHUNK, gw), lambda i: (0, 0, 0))],
        out_specs=pl.BlockSpec((tm, D), lambda i: (i, 0)),
        out_shape=jax.ShapeDtypeStruct((T, D), BF16),
        compiler_params=_params(1, 40),
        name="gmlp_spatial_gating",
    )(proj, proj, gmlp_g.reshape(-1, 1, D), w_s, bs)


def _ctx_attn_kernel(q_ref, k_ref, v_ref, o_ref, *, heads, dh):
    for h in range(heads):
        cols = slice(h * dh, (h + 1) * dh)
        s = lax.dot_general(q_ref[:, cols], k_ref[:, cols], _NT, preferred_element_type=F32)
        p = jnp.exp(s - jnp.max(s, axis=-1, keepdims=True))
        denom = jnp.sum(p, axis=-1, keepdims=True)
        o = jnp.dot(p.astype(BF16), v_ref[:, cols], preferred_element_type=F32)
        o_ref[:, cols] = (o / denom).astype(BF16)


def _context_attention(proj, batch, seq, D, heads, dh):
    T = proj.shape[0]
    return pl.pallas_call(
        functools.partial(_ctx_attn_kernel, heads=heads, dh=dh),
        grid=(batch,),
        in_specs=[pl.BlockSpec((seq, D), lambda b: (b, 0)),
                  pl.BlockSpec((seq, D), lambda b: (b, 1)),
                  pl.BlockSpec((seq, D), lambda b: (b, 2))],
        out_specs=pl.BlockSpec((seq, D), lambda b: (b, 0)),
        out_shape=jax.ShapeDtypeStruct((T, D), BF16),
        compiler_params=_params(1, 32),
        name="context_attention",
    )(proj, proj, proj)


def _na_kernel(q_ref, k_ref, v_ref, kc_ref, vc_ref, u_ref, mk_ref, yb_ref, o_ref, vt_scr, vct_scr, *, hps, dh, rows):
    del yb_ref
    m = pl.program_id(2)
    nblk = rows // NA_QROWS
    win = NA_KROWS * GRID_W
    nkb = win // NA_KBLOCK

    @pl.when(m == 0)
    def _():
        for kb in range(v_ref.shape[0] // NA_KBLOCK):
            vt_scr[kb] = v_ref[kb * NA_KBLOCK:(kb + 1) * NA_KBLOCK, :].astype(F32).T.astype(BF16)
        vct_scr[...] = vc_ref[0].astype(F32).T.astype(BF16)

    r0 = m * NA_QROWS
    ws = jnp.clip(r0 - WIN_ROWS // 2, 0, rows - NA_KROWS)
    krows = pl.ds(pl.multiple_of(ws * GRID_W, NA_KBLOCK), win)
    urows = pl.ds(pl.multiple_of((ws - r0 + WIN_ROWS) * GRID_W, NA_KBLOCK), win)
    kb0 = ws // (NA_KBLOCK // GRID_W)
    row_mask = mk_ref[jnp.where(m == 0, 0, jnp.where(m == nblk - 1, 2, 1))]
    for h in range(hps):
        cols = slice(h * dh, (h + 1) * dh)
        q = q_ref[:, cols]
        s_loc = lax.dot_general(k_ref[krows, cols], q, _NT, preferred_element_type=F32) + u_ref[h, urows, :] + row_mask
        s_ctx = lax.dot_general(kc_ref[0, :, cols], q, _NT, preferred_element_type=F32)
        mx = jnp.maximum(jnp.max(s_loc, axis=0, keepdims=True), jnp.max(s_ctx, axis=0, keepdims=True))
        p_loc = jnp.exp(s_loc - mx)
        p_ctx = jnp.exp(s_ctx - mx)
        denom = jnp.sum(p_loc, axis=0, keepdims=True) + jnp.sum(p_ctx, axis=0, keepdims=True)
        p_loc = p_loc.astype(BF16)
        ot = jnp.dot(vct_scr[cols, :], p_ctx.astype(BF16), preferred_element_type=F32)
        for t in range(nkb):
            ot = ot + jnp.dot(vt_scr[kb0 + t, cols, :], p_loc[t * NA_KBLOCK:(t + 1) * NA_KBLOCK, :],
                              preferred_element_type=F32)
        o_ref[:, cols] = (ot / denom).T.astype(BF16)


def _na_tables(rpb_l, rows):
    kh = WIN_ROWS
    col = jnp.arange(GRID_W)
    cs = jnp.clip(col - WIN_COLS // 2, 0, GRID_W - WIN_COLS)
    col_valid = (col[None, :] >= cs[:, None]) & (col[None, :] < cs[:, None] + WIN_COLS)
    col_idx = jnp.clip(col[None, :] - col[:, None] + WIN_COLS - 1, 0, 2 * WIN_COLS - 2)
    n_rho = NA_KROWS + WIN_ROWS
    rho = jnp.arange(n_rho) - WIN_ROWS
    j = jnp.arange(NA_QROWS)
    dr = jnp.clip(rho[:, None] - j[None, :] + WIN_ROWS - 1, 0, 2 * WIN_ROWS - 2)
    b = rpb_l[:, dr][:, :, :, col_idx]
    b = jnp.where(col_valid[None, None, None], b, -jnp.inf)
    u = b.transpose(0, 1, 4, 2, 3).reshape(rpb_l.shape[0], n_rho * GRID_W, NA_QROWS * GRID_W).astype(F32)

    def band(r0, ws):
        rs = jnp.clip(r0 + j - kh // 2, 0, rows - kh)
        key_row = ws + jnp.arange(NA_KROWS)
        ok = (key_row[:, None] >= rs[None, :]) & (key_row[:, None] < rs[None, :] + kh)
        mk = jnp.where(ok, 0.0, -jnp.inf).astype(F32)
        return jnp.broadcast_to(mk[:, None, :, None], (NA_KROWS, GRID_W, NA_QROWS, GRID_W)).reshape(
            NA_KROWS * GRID_W, NA_QROWS * GRID_W)

    row_mask = jnp.stack([band(0, 0), band(NA_QROWS, 0), band(rows - NA_QROWS, rows - NA_KROWS)])
    return u, row_mask


def _neighbourhood_attention(proj, yb, kc, vc, u, row_mask, nc, dec_batch, dec_seq, D, heads, dh):
    rows = dec_seq // GRID_W
    assert rows % NA_QROWS == 0 and rows >= NA_KROWS
    hps = min(NA_HEADS_PER_STEP, heads)
    qn = NA_QROWS * GRID_W
    cw = hps * dh
    n_hg = heads // hps
    nblk = rows // NA_QROWS
    q0 = nc // qn
    k0 = nc // dec_seq
    ncols = D // cw
    past = kc.shape[1]
    kernel = functools.partial(_na_kernel, hps=hps, dh=dh, rows=rows)
    return pl.pallas_call(
        kernel,
        grid=(dec_batch, n_hg, nblk),
        in_specs=[pl.BlockSpec((qn, cw), lambda b, g, m: (q0 + b * nblk + m, g)),
                  pl.BlockSpec((dec_seq, cw), lambda b, g, m: (k0 + b, ncols + g)),
                  pl.BlockSpec((dec_seq, cw), lambda b, g, m: (k0 + b, 2 * ncols + g)),
                  pl.BlockSpec((1, past, cw), lambda b, g, m: (b, 0, g)),
                  pl.BlockSpec((1, past, cw), lambda b, g, m: (b, 0, g)),
                  pl.BlockSpec((hps,) + u.shape[1:], lambda b, g, m: (g, 0, 0)),
                  pl.BlockSpec(row_mask.shape, lambda b, g, m: (0, 0, 0)),
                  pl.BlockSpec(memory_space=pl.ANY)],
        out_specs=pl.BlockSpec((qn, cw), lambda b, g, m: (q0 + b * nblk + m, g)),
        out_shape=jax.ShapeDtypeStruct(yb.shape, yb.dtype),
        scratch_shapes=[pltpu.VMEM((dec_seq // NA_KBLOCK, cw, NA_KBLOCK), BF16),
                        pltpu.VMEM((cw, past), BF16)],
        input_output_aliases={7: 0},
        compiler_params=_params(3, 52),
        name="neighbourhood_attention",
    )(proj, proj, proj, kc, vc, u, row_mask, yb)


def _merge_kernel(ya_ref, yb_ref, ga_ref, gb_ref, wa_ref, wb_ref, o_ref):
    a = jnp.dot(ya_ref[...], wa_ref[0], preferred_element_type=F32)
    b = jnp.dot(yb_ref[...], wb_ref[0], preferred_element_type=F32)
    o = _sigmoid(ga_ref[...].astype(F32)) * a + _sigmoid(gb_ref[...].astype(F32)) * b
    o_ref[...] = o.astype(BF16)


def _merge(ya, yb, proj, w_pa, w_pb, l, D, tm):
    T = ya.shape[0]
    tn = _tile(D, 512)
    nj = D // tn
    return pl.pallas_call(
        _merge_kernel,
        grid=(T // tm, nj),
        in_specs=[pl.BlockSpec((tm, D), lambda i, j: (i, 0)),
                  pl.BlockSpec((tm, D), lambda i, j: (i, 0)),
                  pl.BlockSpec((tm, tn), lambda i, j: (i, 5 * nj + j)),
                  pl.BlockSpec((tm, tn), lambda i, j: (i, 6 * nj + j)),
                  pl.BlockSpec((1, D, tn), lambda i, j: (l, 0, j)),
                  pl.BlockSpec((1, D, tn), lambda i, j: (l, 0, j))],
        out_specs=pl.BlockSpec((tm, tn), lambda i, j: (i, j)),
        out_shape=jax.ShapeDtypeStruct((T, D), BF16),
        compiler_params=_params(2, 48),
        name="gated_branch_merge",
    )(ya, yb, proj, proj, w_pa, w_pb)


def _argmax_first(vals):
    best = vals[0]
    idx = jnp.zeros(best.shape, I32)
    for k in range(1, len(vals)):
        upd = vals[k] > best
        idx = jnp.where(upd, k, idx)
        best = jnp.where(upd, vals[k], best)
    return idx


def _select(idx, vals):
    out = vals[-1]
    for k in range(len(vals) - 2, -1, -1):
        out = jnp.where(idx == k, vals[k], out)
    return out


def _route(logits_t, router_b):
    m = jnp.max(logits_t, axis=0, keepdims=True)
    ex = jnp.exp(logits_t - m)
    probs = ex / jnp.sum(ex, axis=0, keepdims=True)
    sel = probs + router_b
    sel_rows = [sel[e:e + 1, :] for e in range(N_EXPERTS)]
    prob_rows = [probs[e:e + 1, :] for e in range(N_EXPERTS)]
    scores = []
    for g in range(N_EXPERT_GROUPS):
        a, b, c, d = sel_rows[g * EXPERTS_PER_GROUP:(g + 1) * EXPERTS_PER_GROUP]
        hi1, lo1 = jnp.maximum(a, b), jnp.minimum(a, b)
        hi2, lo2 = jnp.maximum(c, d), jnp.minimum(c, d)
        top1 = jnp.maximum(hi1, hi2)
        top2 = jnp.maximum(jnp.minimum(hi1, hi2), jnp.where(hi1 >= hi2, lo1, lo2))
        scores.append(top1 + top2)
    best_g = _argmax_first(scores)
    in_sel = [_select(best_g, [sel_rows[g * EXPERTS_PER_GROUP + k] for g in range(N_EXPERT_GROUPS)])
              for k in range(EXPERTS_PER_GROUP)]
    in_prob = [_select(best_g, [prob_rows[g * EXPERTS_PER_GROUP + k] for g in range(N_EXPERT_GROUPS)])
               for k in range(EXPERTS_PER_GROUP)]
    i1 = _argmax_first(in_sel)
    i2 = _argmax_first([jnp.where(i1 == k, -jnp.inf, in_sel[k]) for k in range(EXPERTS_PER_GROUP)])
    w1 = _select(i1, in_prob)
    w2 = _select(i2, in_prob)
    tot = w1 + w2
    base = best_g * EXPERTS_PER_GROUP
    return (base + i1, base + i2), (w1 / tot, w2 / tot)


def _out_kernel(mg_ref, x_ref, mod_ref, g_ref, w_ref, rw_ref, rb_ref, xo_ref, h_ref, e_ref, gt_ref, acc_scr, hf_scr):
    acc_scr[...] = jnp.dot(mg_ref[...], w_ref[0], preferred_element_type=F32)
    gate1 = mod_ref[0, 0, 2:3, :]
    shift2 = mod_ref[0, 0, 3:4, :]
    scale2 = 1.0 + mod_ref[0, 0, 4:5, :]
    g = g_ref[0]

    def chunk(rows):
        xn = x_ref[rows, :] + gate1 * acc_scr[rows, :]
        xo_ref[rows, :] = xn
        h = _rms_scale(xn) * g * scale2 + shift2
        hf_scr[rows, :] = h
        h_ref[rows, :] = h.astype(BF16)

    _for_chunks(x_ref.shape[0], ROW_CHUNK, chunk)
    logits_t = lax.dot_general(rw_ref[...], hf_scr[...], _NT, preferred_element_type=F32,
                               precision=lax.Precision.HIGHEST)
    experts, gates = _route(logits_t, rb_ref[...])
    for k in range(TOP_K):
        e_ref[k:k + 1, :] = experts[k]
        gt_ref[k:k + 1, :] = gates[k]


def _output_projection(merged, x, mod, norm_g, w_out, router_wt, router_b, l, cond_of_block, tm):
    T, D = x.shape
    E = router_wt.shape[0]
    return pl.pallas_call(
        _out_kernel,
        grid=(T // tm,),
        in_specs=[pl.BlockSpec((tm, D), lambda i: (i, 0)),
                  pl.BlockSpec((tm, D), lambda i: (i, 0)),
                  pl.BlockSpec((1, 1, N_MOD, D), lambda i: (l, cond_of_block(i), 0, 0)),
                  pl.BlockSpec((1, 1, D), lambda i: (l, 0, 0)),
                  pl.BlockSpec((1, D, D), lambda i: (l, 0, 0)),
                  pl.BlockSpec((E, D), lambda i: (0, 0)),
                  pl.BlockSpec((E, 1), lambda i: (0, 0))],
        out_specs=[pl.BlockSpec((tm, D), lambda i: (i, 0)),
                   pl.BlockSpec((tm, D), lambda i: (i, 0)),
                   pl.BlockSpec((TOP_K, tm), lambda i: (0, i)),
                   pl.BlockSpec((TOP_K, tm), lambda i: (0, i))],
        out_shape=[jax.ShapeDtypeStruct((T, D), F32),
                   jax.ShapeDtypeStruct((T, D), BF16),
                   jax.ShapeDtypeStruct((TOP_K, T), I32),
                   jax.ShapeDtypeStruct((TOP_K, T), F32)],
        scratch_shapes=[pltpu.VMEM((tm, D), F32), pltpu.VMEM((tm, D), F32)],
        compiler_params=_params(1, 56),
        name="output_projection_norm2_router",
    )(merged, x, mod, norm_g.reshape(-1, 1, D), w_out, router_wt, router_b.reshape(E, 1))


def _moe_kernel(blk_ref, exp_ref, lo_ref, hi_ref, x_ref, wg_ref, wu_ref, wd_ref, o_ref):
    del exp_ref
    w = pl.program_id(0)
    lo, hi = lo_ref[w], hi_ref[w]
    row0 = blk_ref[w] * MOE_BLOCK

    @pl.when(lo == row0)
    def _():
        o_ref[...] = jnp.zeros(o_ref.shape, o_ref.dtype)

    @pl.when(hi > lo)
    def _():
        x = x_ref[...]
        g = jnp.dot(x, wg_ref[0], preferred_element_type=F32)
        u = jnp.dot(x, wu_ref[0], preferred_element_type=F32)
        hid = (g * _sigmoid(g)) * u
        y = jnp.dot(hid.astype(BF16), wd_ref[0], preferred_element_type=F32)
        row = row0 + lax.broadcasted_iota(I32, (MOE_BLOCK, 1), 0)
        o_ref[...] += jnp.where((row >= lo) & (row < hi), y, 0.0)


def _moe_experts(xs, items, w_gate, w_up, w_down, l):
    A, D = xs.shape
    E, DE = w_gate.shape[1], w_gate.shape[3]
    w_gate = w_gate.reshape(-1, D, DE)
    w_up = w_up.reshape(-1, D, DE)
    w_down = w_down.reshape(-1, DE, D)
    n_items = items[0].shape[0]
    grid_spec = pltpu.PrefetchScalarGridSpec(
        num_scalar_prefetch=4,
        grid=(n_items,),
        in_specs=[pl.BlockSpec((MOE_BLOCK, D), lambda w, blk, ex, lo, hi: (blk[w], 0)),
                  pl.BlockSpec((1, D, DE), lambda w, blk, ex, lo, hi: (l * E + ex[w], 0, 0)),
                  pl.BlockSpec((1, D, DE), lambda w, blk, ex, lo, hi: (l * E + ex[w], 0, 0)),
                  pl.BlockSpec((1, DE, D), lambda w, blk, ex, lo, hi: (l * E + ex[w], 0, 0))],
        out_specs=pl.BlockSpec((MOE_BLOCK, D), lambda w, blk, ex, lo, hi: (blk[w], 0)),
    )
    return pl.pallas_call(
        _moe_kernel,
        grid_spec=grid_spec,
        out_shape=jax.ShapeDtypeStruct((A, D), F32),
        compiler_params=_params(1, 48),
        name="moe_experts",
    )(*items, xs, w_gate, w_up, w_down)


def _dispatch_plan(experts, T):
    A = TOP_K * T
    nb = A // MOE_BLOCK
    n_items = nb + N_EXPERTS - 1
    e_flat = experts.reshape(A)
    iota = jnp.arange(A, dtype=I32)
    _, order = lax.sort((e_flat, iota), num_keys=1, is_stable=True)
    _, rank = lax.sort((order, iota), num_keys=1)
    eid = jnp.arange(N_EXPERTS, dtype=I32)
    counts = jnp.sum((e_flat[:, None] == eid[None, :]).astype(I32), axis=0)
    ends = jnp.cumsum(counts)
    starts = ends - counts
    first_blk = starts // MOE_BLOCK
    n_blk = jnp.where(counts > 0, (ends - 1) // MOE_BLOCK - first_blk + 1, 0)
    item_end = jnp.cumsum(n_blk)
    item_start = item_end - n_blk
    w = jnp.arange(n_items, dtype=I32)
    valid = w < item_end[-1]
    ex = jnp.minimum(jnp.sum((w[:, None] >= item_end[None, :]).astype(I32), axis=1), N_EXPERTS - 1)
    pick = lambda tbl: jnp.sum(jnp.where(ex[:, None] == eid[None, :], tbl[None, :], 0), axis=1)
    blk = jnp.where(valid, pick(first_blk) + w - pick(item_start), nb - 1)
    lo = jnp.where(valid, jnp.maximum(pick(starts), blk * MOE_BLOCK), 0)
    hi = jnp.where(valid, jnp.minimum(pick(ends), (blk + 1) * MOE_BLOCK), 0)
    ex = jnp.where(valid, ex, jnp.max(jnp.where(counts > 0, eid, 0)))
    return order % T, rank, (blk.astype(I32), ex.astype(I32), lo.astype(I32), hi.astype(I32))


def _combine_kernel(x_ref, y1_ref, y2_ref, g1_ref, g2_ref, mod_ref, o_ref):
    gate2 = mod_ref[0, 0, 5:6, :]

    def chunk(rows):
        y = g1_ref[rows, :] * y1_ref[rows, :] + g2_ref[rows, :] * y2_ref[rows, :]
        o_ref[rows, :] = x_ref[rows, :] + gate2 * y

    _for_chunks(x_ref.shape[0], ROW_CHUNK, chunk)


def _combine(x, y1, y2, g1, g2, mod, l, cond_of_block, tm):
    T, D = x.shape
    row = pl.BlockSpec((tm, D), lambda i: (i, 0))
    col = pl.BlockSpec((tm, 1), lambda i: (i, 0))
    return pl.pallas_call(
        _combine_kernel,
        grid=(T // tm,),
        in_specs=[row, row, row, col, col,
                  pl.BlockSpec((1, 1, N_MOD, D), lambda i: (l, cond_of_block(i), 0, 0))],
        out_specs=row,
        out_shape=jax.ShapeDtypeStruct((T, D), F32),
        compiler_params=_params(1, 48),
        name="moe_combine_residual",
    )(x, y1, y2, g1, g2, mod)


def _final_norm_kernel(x_ref, g_ref, o_ref):
    g = g_ref[...]

    def chunk(rows):
        o_ref[rows, :] = _rms_scale(x_ref[rows, :]) * g

    _for_chunks(x_ref.shape[0], ROW_CHUNK, chunk)


def _final_norm(x, g, row0, n_rows, tm):
    D = x.shape[1]
    b0 = row0 // tm
    return pl.pallas_call(
        _final_norm_kernel,
        grid=(n_rows // tm,),
        in_specs=[pl.BlockSpec((tm, D), lambda i: (b0 + i, 0)),
                  pl.BlockSpec((1, D), lambda i: (0, 0))],
        out_specs=pl.BlockSpec((tm, D), lambda i: (i, 0)),
        out_shape=jax.ShapeDtypeStruct((n_rows, D), F32),
        compiler_params=_params(1, 32),
        name="final_norm",
    )(x, g.reshape(1, D))


def kernel(x_prompt, x_sample, cache_k, cache_v, c, c_ctx, w_ada, b_ada, norm1_g, w_in, gmlp_g, w_s, b_s, rpb,
           w_proj_a, w_proj_b, w_out, norm2_g, router_w, router_b, w_gate, w_up, w_down, final_g):
    batch, seq, D = x_prompt.shape
    dec_batch, dec_seq, _ = x_sample.shape
    depth = w_in.shape[0]
    heads, dh = cache_k.shape[3], cache_k.shape[4]
    past = cache_k.shape[2]
    nc, nl = batch * seq, dec_batch * dec_seq
    T = nc + nl
    assert heads * dh == D and gmlp_g.shape[1] == D and w_in.shape[2] == 7 * D
    assert 1 + dec_batch <= COND_ROWS
    assert router_w.shape[1] == N_EXPERTS
    assert seq % GMLP_CHUNK == 0 and dec_seq % GMLP_CHUNK == 0 and dec_seq % GRID_W == 0
    assert nc % dec_seq == 0 and (TOP_K * T) % MOE_BLOCK == 0

    tm_big = 1024 if (nc % 1024 == 0 and dec_seq % 1024 == 0) else GMLP_CHUNK
    tm_mid = 512 if (nc % 512 == 0 and dec_seq % 512 == 0) else GMLP_CHUNK

    def cond_of(tm):
        ncb, per = nc // tm, dec_seq // tm
        return lambda i: jnp.where(i < ncb, 0, 1 + (i - ncb) // per)

    cond = jnp.zeros((COND_ROWS, D), F32).at[0].set(c_ctx).at[1:1 + dec_batch].set(c)
    mod = _modulation_all(cond, w_ada, b_ada)

    w_in_b = w_in.astype(BF16)
    w_s_b = w_s.astype(BF16)
    w_pa_b = w_proj_a.astype(BF16)
    w_pb_b = w_proj_b.astype(BF16)
    w_out_b = w_out.astype(BF16)
    w_gate_b = w_gate.astype(BF16)
    w_up_b = w_up.astype(BF16)
    w_down_b = w_down.astype(BF16)
    kc_all = cache_k.astype(BF16).reshape(dec_batch, depth, past, D)
    vc_all = cache_v.astype(BF16).reshape(dec_batch, depth, past, D)
    router_wt = router_w.T
    rows = dec_seq // GRID_W

    x = jnp.concatenate([x_prompt.reshape(nc, D), x_sample.reshape(nl, D)], axis=0)
    new_k, new_v = [], []
    for l in range(depth):
        proj, k_ctx, v_ctx = _input_projection(x, mod, norm1_g, w_in_b, l, nc, cond_of(tm_big), tm_big, dh ** -0.5)
        new_k.append(k_ctx.reshape(batch, seq, heads, dh))
        new_v.append(v_ctx.reshape(batch, seq, heads, dh))
        ya = _gmlp(proj, gmlp_g, w_s_b, b_s, l, D, tm_mid)
        yb = _context_attention(proj, batch, seq, D, heads, dh)
        u, row_mask = _na_tables(rpb[l], rows)
        yb = _neighbourhood_attention(proj, yb, kc_all[:, l], vc_all[:, l], u, row_mask, nc, dec_batch, dec_seq, D,
                                      heads, dh)
        merged = _merge(ya, yb, proj, w_pa_b, w_pb_b, l, D, tm_big)
        x, h2, experts, gates = _output_projection(merged, x, mod, norm2_g, w_out_b, router_wt, router_b, l,
                                                   cond_of(tm_mid), tm_mid)
        tok_sorted, rank, items = _dispatch_plan(experts, T)
        yexp = _moe_experts(h2[tok_sorted], items, w_gate_b, w_up_b, w_down_b, l)
        x = _combine(x, yexp[rank[:T]], yexp[rank[T:]], gates[0][:, None], gates[1][:, None], mod, l,
                     cond_of(tm_mid), tm_mid)

    y_prompt = _final_norm(x, final_g, 0, nc, tm_mid).reshape(batch, seq, D)
    y_sample = _final_norm(x, final_g, nc, nl, tm_mid).reshape(dec_batch, dec_seq, D)
    return (y_prompt, y_sample, jnp.stack(new_k, axis=1), jnp.stack(new_v, axis=1))
```

```python
import functools

import numpy as np
import jax
import jax.numpy as jnp
from jax import lax
from jax.experimental import pallas as pl
from jax.experimental.pallas import tpu as pltpu

F32 = jnp.float32
BF16 = jnp.bfloat16
I32 = jnp.int32

EPS = 1e-6
N_MOD = 6
GRID_W = 64
WIN_ROWS = 8
WIN_COLS = 16
GMLP_CHUNK = 128
GMLP_GROUPS = 8
N_EXPERTS = 16
N_EXPERT_GROUPS = 4
EXPERTS_PER_GROUP = N_EXPERTS // N_EXPERT_GROUPS
TOP_K = 2
MOE_BLOCK = 256
COND_ROWS = 8
ROW_CHUNK = 64
CAST_CHUNK = 256
NA_QROWS = 4
NA_KROWS = 12
NA_KBLOCK = 256
NA_HEADS_PER_STEP = 4
MIB = 1024 * 1024

_NT = (((1,), (1,)), ((), ()))


def _tile(n, pref):
    return pref if n % pref == 0 else n


def _params(n_axes, vmem_mib):
    return pltpu.CompilerParams(dimension_semantics=("arbitrary",) * n_axes,
                                vmem_limit_bytes=vmem_mib * MIB)


def _for_chunks(n, c, fn):
    c = min(c, n)

    def body(t, carry):
        fn(pl.ds(pl.multiple_of(t * c, c), c))
        return carry

    lax.fori_loop(0, n // c, body, 0)


def _sigmoid(x):
    return 1.0 / (1.0 + jnp.exp(-x))


def _gelu_tanh(x):
    return x * (0.5 * (1.0 + jnp.tanh(0.7978845608028654 * (x + 0.044715 * (x * x * x)))))


def _rms_scale(x):
    return x * lax.rsqrt(jnp.mean(x * x, axis=-1, keepdims=True) + EPS)


def _ada_kernel(c_ref, w_ref, b_ref, o_ref):
    c = c_ref[...]
    s = (c * _sigmoid(c)).astype(BF16)
    o_ref[0] = jnp.dot(s, w_ref[0].astype(BF16), preferred_element_type=F32) + b_ref[0]


def _modulation_all(cond, w_ada, b_ada):
    L, D, N = w_ada.shape
    tn = _tile(N, 1024)
    out = pl.pallas_call(
        _ada_kernel,
        grid=(L, N // tn),
        in_specs=[pl.BlockSpec((COND_ROWS, D), lambda l, j: (0, 0)),
                  pl.BlockSpec((1, D, tn), lambda l, j: (l, 0, j)),
                  pl.BlockSpec((1, 1, tn), lambda l, j: (l, 0, j))],
        out_specs=pl.BlockSpec((1, COND_ROWS, tn), lambda l, j: (l, 0, j)),
        out_shape=jax.ShapeDtypeStruct((L, COND_ROWS, N), F32),
        compiler_params=_params(2, 40),
        name="ada_modulation",
    )(cond, w_ada, b_ada.reshape(L, 1, N))
    return out.reshape(L, COND_ROWS, N_MOD, D)


def _prologue_kernel(xp_ref, xs_ref, mod_ref, g_ref, x_ref, h_ref, *, ncb):
    i = pl.program_id(0)
    g = g_ref[0]
    scale1 = 1.0 + mod_ref[0, 0, 1:2, :]
    shift1 = mod_ref[0, 0, 0:1, :]

    def run(src_ref):
        def chunk(rows):
            x = src_ref[rows, :]
            x_ref[rows, :] = x
            h_ref[rows, :] = (_rms_scale(x) * g * scale1 + shift1).astype(BF16)

        _for_chunks(x_ref.shape[0], ROW_CHUNK, chunk)

    pl.when(i < ncb)(lambda: run(xp_ref))
    pl.when(i >= ncb)(lambda: run(xs_ref))


def _prologue(xp, xs, mod, norm_g, cond_of_block, tm):
    nc, D = xp.shape
    T = nc + xs.shape[0]
    ncb = nc // tm
    row = pl.BlockSpec((tm, D), lambda i: (i, 0))
    return pl.pallas_call(
        functools.partial(_prologue_kernel, ncb=ncb),
        grid=(T // tm,),
        in_specs=[pl.BlockSpec((tm, D), lambda i: (jnp.minimum(i, ncb - 1), 0)),
                  pl.BlockSpec((tm, D), lambda i: (jnp.maximum(i - ncb, 0), 0)),
                  pl.BlockSpec((1, 1, N_MOD, D), lambda i: (0, cond_of_block(i), 0, 0)),
                  pl.BlockSpec((1, 1, D), lambda i: (0, 0, 0))],
        out_specs=[row, row],
        out_shape=[jax.ShapeDtypeStruct((T, D), F32), jax.ShapeDtypeStruct((T, D), BF16)],
        compiler_params=_params(1, 40),
        name="concat_norm1",
    )(xp, xs, mod, norm_g.reshape(-1, 1, D))


def _proj_kernel(h_ref, w_ref, o_ref, kv_ref, w_scr, *, ncb, jk0, nkv, qscale):
    j = pl.program_id(0)
    i = pl.program_id(1)

    @pl.when(i == 0)
    def _():
        def chunk(rows):
            w_scr[rows, :] = w_ref[0, rows, :].astype(BF16)

        _for_chunks(w_scr.shape[0], CAST_CHUNK, chunk)

    acc = jnp.dot(h_ref[...], w_scr[...], preferred_element_type=F32)
    o_ref[...] = (acc * jnp.where(j < jk0, qscale, 1.0)).astype(BF16)

    @pl.when((i < ncb) & (j >= jk0) & (j < jk0 + nkv))
    def _():
        kv_ref[...] = acc


def _input_projection(h, w_in, l, nc, tm, qscale):
    T, D = h.shape
    N = w_in.shape[-1]
    tn = _tile(D, 1024)
    ncb = nc // tm
    jk0, nkv = D // tn, 2 * D // tn
    kernel = functools.partial(_proj_kernel, ncb=ncb, jk0=jk0, nkv=nkv, qscale=qscale)

    def kv_index(j, i):
        inside = (j >= jk0) & (j < jk0 + nkv)
        row = jnp.where(j < jk0, 0, jnp.where(inside, jnp.minimum(i, ncb - 1), ncb - 1))
        col = jnp.where(j < jk0, 0, jnp.where(inside, j - jk0, nkv - 1))
        return (row, col)

    return pl.pallas_call(
        kernel,
        grid=(N // tn, T // tm),
        in_specs=[pl.BlockSpec((tm, D), lambda j, i: (i, 0)),
                  pl.BlockSpec((1, D, tn), lambda j, i: (l, 0, j))],
        out_specs=[pl.BlockSpec((tm, tn), lambda j, i: (i, j)),
                   pl.BlockSpec((tm, tn), kv_index)],
        out_shape=[jax.ShapeDtypeStruct((T, N), BF16),
                   jax.ShapeDtypeStruct((nc, 2 * D), F32)],
        scratch_shapes=[pltpu.VMEM((D, tn), BF16)],
        compiler_params=_params(2, 52),
        name="input_projection",
    )(h, w_in)


def _gmlp_kernel(u_ref, gv_ref, g_ref, ws_ref, bs_ref, o_ref):
    gw = u_ref.shape[1] // GMLP_GROUPS
    g = g_ref[0]

    def chunk(rows):
        v = _rms_scale(_gelu_tanh(gv_ref[rows, :].astype(F32))) * g
        v = v.astype(BF16)
        for grp in range(GMLP_GROUPS):
            cols = slice(grp * gw, (grp + 1) * gw)
            s = jnp.dot(ws_ref[grp], v[:, cols], preferred_element_type=F32) + bs_ref[grp]
            u = _gelu_tanh(u_ref[rows, cols].astype(F32))
            o_ref[rows, cols] = (u * s).astype(BF16)

    _for_chunks(u_ref.shape[0], GMLP_CHUNK, chunk)


def _gmlp(proj, gmlp_g, w_s, b_s, l, D, tm):
    T = proj.shape[0]
    gw = D // GMLP_GROUPS
    bs = jnp.broadcast_to(b_s[l][:, :, None], (GMLP_GROUPS, GMLP_CHUNK, gw))
    return pl.pallas_call(
        _gmlp_kernel,
        grid=(T // tm,),
        in_specs=[pl.BlockSpec((tm, D), lambda i: (i, 3)),
                  pl.BlockSpec((tm, D), lambda i: (i, 4)),
                  pl.BlockSpec((1, 1, D), lambda i: (l, 0, 0)),
                  pl.BlockSpec((None, GMLP_GROUPS, GMLP_CHUNK, GMLP_CHUNK), lambda i: (l, 0, 0, 0)),
                  pl.BlockSpec((GMLP_GROUPS, GMLP_CHUNK, gw), lambda i: (0, 0, 0))],
        out_specs=pl.BlockSpec((tm, D), lambda i: (i, 0)),
        out_shape=jax.ShapeDtypeStruct((T, D), BF16),
        compiler_params=_params(1, 40),
        name="gmlp_spatial_gating",
    )(proj, proj, gmlp_g.reshape(-1, 1, D), w_s, bs)


def _ctx_attn_kernel(q_ref, k_ref, v_ref, o_ref, *, heads, dh):
    for h in range(heads):
        cols = slice(h * dh, (h + 1) * dh)
        s = lax.dot_general(q_ref[:, cols], k_ref[:, cols], _NT, preferred_element_type=F32)
        p = jnp.exp(s - jnp.max(s, axis=-1, keepdims=True))
        denom = jnp.sum(p, axis=-1, keepdims=True)
        o = jnp.dot(p.astype(BF16), v_ref[:, cols], preferred_element_type=F32)
        o_ref[:, cols] = (o / denom).astype(BF16)


def _context_attention(proj, batch, seq, D, heads, dh):
    return pl.pallas_call(
        functools.partial(_ctx_attn_kernel, heads=heads, dh=dh),
        grid=(batch,),
        in_specs=[pl.BlockSpec((seq, D), lambda b: (b, 0)),
                  pl.BlockSpec((seq, D), lambda b: (b, 1)),
                  pl.BlockSpec((seq, D), lambda b: (b, 2))],
        out_specs=pl.BlockSpec((seq, D), lambda b: (b, 0)),
        out_shape=jax.ShapeDtypeStruct((batch * seq, D), BF16),
        compiler_params=_params(1, 32),
        name="context_attention",
    )(proj, proj, proj)


def _na_kernel(q_ref, k_ref, v_ref, kc_ref, vc_ref, u_ref, mk_ref, o_ref, vt_scr, vct_scr, *, hps, dh, rows):
    m = pl.program_id(2)
    nblk = rows // NA_QROWS
    win = NA_KROWS * GRID_W
    nkb = win // NA_KBLOCK

    @pl.when(m == 0)
    def _():
        for kb in range(v_ref.shape[0] // NA_KBLOCK):
            vt_scr[kb] = v_ref[kb * NA_KBLOCK:(kb + 1) * NA_KBLOCK, :].astype(F32).T.astype(BF16)
        vct_scr[...] = vc_ref[0].astype(F32).T.astype(BF16)

    r0 = m * NA_QROWS
    ws = jnp.clip(r0 - WIN_ROWS // 2, 0, rows - NA_KROWS)
    krows = pl.ds(pl.multiple_of(ws * GRID_W, NA_KBLOCK), win)
    urows = pl.ds(pl.multiple_of((ws - r0 + WIN_ROWS) * GRID_W, NA_KBLOCK), win)
    kb0 = ws // (NA_KBLOCK // GRID_W)
    row_mask = mk_ref[jnp.where(m == 0, 0, jnp.where(m == nblk - 1, 2, 1))]
    for h in range(hps):
        cols = slice(h * dh, (h + 1) * dh)
        q = q_ref[:, cols]
        s_loc = (lax.dot_general(k_ref[krows, cols], q, _NT, preferred_element_type=F32)
                 + u_ref[0, h, urows, :] + row_mask)
        s_ctx = lax.dot_general(kc_ref[0, :, cols], q, _NT, preferred_element_type=F32)
        mx = jnp.maximum(jnp.max(s_loc, axis=0, keepdims=True), jnp.max(s_ctx, axis=0, keepdims=True))
        p_loc = jnp.exp(s_loc - mx)
        p_ctx = jnp.exp(s_ctx - mx)
        denom = jnp.sum(p_loc, axis=0, keepdims=True) + jnp.sum(p_ctx, axis=0, keepdims=True)
        p_loc = p_loc.astype(BF16)
        ot = jnp.dot(vct_scr[cols, :], p_ctx.astype(BF16), preferred_element_type=F32)
        for t in range(nkb):
            ot = ot + jnp.dot(vt_scr[kb0 + t, cols, :], p_loc[t * NA_KBLOCK:(t + 1) * NA_KBLOCK, :],
                              preferred_element_type=F32)
        o_ref[:, cols] = (ot / denom).T.astype(BF16)


def _na_tables(rpb, rows):
    L, H = rpb.shape[:2]
    kh = WIN_ROWS
    col = np.arange(GRID_W)
    cs = np.clip(col - WIN_COLS // 2, 0, GRID_W - WIN_COLS)
    col_valid = (col[:, None] >= cs[None, :]) & (col[:, None] < cs[None, :] + WIN_COLS)
    col_idx = np.clip(col[:, None] - col[None, :] + WIN_COLS - 1, 0, 2 * WIN_COLS - 2)
    n_rho = NA_KROWS + WIN_ROWS
    j = np.arange(NA_QROWS)
    dr = np.clip((np.arange(n_rho) - WIN_ROWS)[:, None] - j[None, :] + WIN_ROWS - 1, 0, 2 * WIN_ROWS - 2)
    t1 = jnp.where(col_valid, rpb[..., col_idx], -jnp.inf).astype(F32)
    u = t1[:, :, dr]
    u = u.transpose(0, 1, 2, 4, 3, 5).reshape(L, H, n_rho * GRID_W, NA_QROWS * GRID_W)

    def band(r0, ws):
        rs = np.clip(r0 + j - kh // 2, 0, rows - kh)
        key_row = ws + np.arange(NA_KROWS)
        ok = (key_row[:, None] >= rs[None, :]) & (key_row[:, None] < rs[None, :] + kh)
        mk = np.where(ok, 0.0, -np.inf).astype(np.float32)
        return np.broadcast_to(mk[:, None, :, None], (NA_KROWS, GRID_W, NA_QROWS, GRID_W)).reshape(
            NA_KROWS * GRID_W, NA_QROWS * GRID_W)

    row_mask = np.stack([band(0, 0), band(NA_QROWS, 0), band(rows - NA_QROWS, rows - NA_KROWS)])
    return u, jnp.asarray(row_mask)


def _neighbourhood_attention(proj, kc, vc, u, row_mask, l, nc, dec_batch, dec_seq, D, heads, dh):
    rows = dec_seq // GRID_W
    assert rows % NA_QROWS == 0 and rows >= NA_KROWS
    hps = min(NA_HEADS_PER_STEP, heads)
    qn = NA_QROWS * GRID_W
    cw = hps * dh
    n_hg = heads // hps
    nblk = rows // NA_QROWS
    q0 = nc // qn
    k0 = nc // dec_seq
    ncols = D // cw
    past = kc.shape[1]
    kernel = functools.partial(_na_kernel, hps=hps, dh=dh, rows=rows)
    return pl.pallas_call(
        kernel,
        grid=(dec_batch, n_hg, nblk),
        in_specs=[pl.BlockSpec((qn, cw), lambda b, g, m: (q0 + b * nblk + m, g)),
                  pl.BlockSpec((dec_seq, cw), lambda b, g, m: (k0 + b, ncols + g)),
                  pl.BlockSpec((dec_seq, cw), lambda b, g, m: (k0 + b, 2 * ncols + g)),
                  pl.BlockSpec((1, past, cw), lambda b, g, m: (b, 0, g)),
                  pl.BlockSpec((1, past, cw), lambda b, g, m: (b, 0, g)),
                  pl.BlockSpec((1, hps) + u.shape[2:], lambda b, g, m: (l, g, 0, 0)),
                  pl.BlockSpec(row_mask.shape, lambda b, g, m: (0, 0, 0))],
        out_specs=pl.BlockSpec((qn, cw), lambda b, g, m: (b * nblk + m, g)),
        out_shape=jax.ShapeDtypeStruct((dec_batch * dec_seq, D), BF16),
        scratch_shapes=[pltpu.VMEM((dec_seq // NA_KBLOCK, cw, NA_KBLOCK), BF16),
                        pltpu.VMEM((cw, past), BF16)],
        compiler_params=_params(3, 52),
        name="neighbourhood_attention",
    )(proj, proj, proj, kc, vc, u, row_mask)


def _merge_kernel(ya_ref, ybc_ref, ybl_ref, ga_ref, gb_ref, wa_ref, wb_ref, o_ref, b_scr, *, ncb):
    i = pl.program_id(0)

    @pl.when(i < ncb)
    def _():
        b_scr[...] = jnp.dot(ybc_ref[...], wb_ref[0], preferred_element_type=F32)

    @pl.when(i >= ncb)
    def _():
        b_scr[...] = jnp.dot(ybl_ref[...], wb_ref[0], preferred_element_type=F32)

    a = jnp.dot(ya_ref[...], wa_ref[0], preferred_element_type=F32)
    o = _sigmoid(ga_ref[...].astype(F32)) * a + _sigmoid(gb_ref[...].astype(F32)) * b_scr[...]
    o_ref[...] = o.astype(BF16)


def _merge(ya, yb_ctx, yb_lat, proj, w_pa, w_pb, l, D, tm):
    T = ya.shape[0]
    tn = _tile(D, 512)
    nj = D // tn
    ncb = yb_ctx.shape[0] // tm
    return pl.pallas_call(
        functools.partial(_merge_kernel, ncb=ncb),
        grid=(T // tm, nj),
        in_specs=[pl.BlockSpec((tm, D), lambda i, j: (i, 0)),
                  pl.BlockSpec((tm, D), lambda i, j: (jnp.minimum(i, ncb - 1), 0)),
                  pl.BlockSpec((tm, D), lambda i, j: (jnp.maximum(i - ncb, 0), 0)),
                  pl.BlockSpec((tm, tn), lambda i, j: (i, 5 * nj + j)),
                  pl.BlockSpec((tm, tn), lambda i, j: (i, 6 * nj + j)),
                  pl.BlockSpec((1, D, tn), lambda i, j: (l, 0, j)),
                  pl.BlockSpec((1, D, tn), lambda i, j: (l, 0, j))],
        out_specs=pl.BlockSpec((tm, tn), lambda i, j: (i, j)),
        out_shape=jax.ShapeDtypeStruct((T, D), BF16),
        scratch_shapes=[pltpu.VMEM((tm, tn), F32)],
        compiler_params=_params(2, 56),
        name="gated_branch_merge",
    )(ya, yb_ctx, yb_lat, proj, proj, w_pa, w_pb)


def _argmax_first(vals):
    best = vals[0]
    idx = jnp.zeros(best.shape, I32)
    for k in range(1, len(vals)):
        upd = vals[k] > best
        idx = jnp.where(upd, k, idx)
        best = jnp.where(upd, vals[k], best)
    return idx


def _select(idx, vals):
    out = vals[-1]
    for k in range(len(vals) - 2, -1, -1):
        out = jnp.where(idx == k, vals[k], out)
    return out


def _route(logits_t, router_b):
    m = jnp.max(logits_t, axis=0, keepdims=True)
    ex = jnp.exp(logits_t - m)
    probs = ex / jnp.sum(ex, axis=0, keepdims=True)
    sel = probs + router_b
    sel_rows = [sel[e:e + 1, :] for e in range(N_EXPERTS)]
    prob_rows = [probs[e:e + 1, :] for e in range(N_EXPERTS)]
    scores = []
    for g in range(N_EXPERT_GROUPS):
        a, b, c, d = sel_rows[g * EXPERTS_PER_GROUP:(g + 1) * EXPERTS_PER_GROUP]
        hi1, lo1 = jnp.maximum(a, b), jnp.minimum(a, b)
        hi2, lo2 = jnp.maximum(c, d), jnp.minimum(c, d)
        top1 = jnp.maximum(hi1, hi2)
        top2 = jnp.maximum(jnp.minimum(hi1, hi2), jnp.where(hi1 >= hi2, lo1, lo2))
        scores.append(top1 + top2)
    best_g = _argmax_first(scores)
    in_sel = [_select(best_g, [sel_rows[g * EXPERTS_PER_GROUP + k] for g in range(N_EXPERT_GROUPS)])
              for k in range(EXPERTS_PER_GROUP)]
    in_prob = [_select(best_g, [prob_rows[g * EXPERTS_PER_GROUP + k] for g in range(N_EXPERT_GROUPS)])
               for k in range(EXPERTS_PER_GROUP)]
    i1 = _argmax_first(in_sel)
    i2 = _argmax_first([jnp.where(i1 == k, -jnp.inf, in_sel[k]) for k in range(EXPERTS_PER_GROUP)])
    w1 = _select(i1, in_prob)
    w2 = _select(i2, in_prob)
    tot = w1 + w2
    base = best_g * EXPERTS_PER_GROUP
    return (base + i1, base + i2), (w1 / tot, w2 / tot)


def _out_kernel(mg_ref, x_ref, mod_ref, g_ref, w_ref, rw_ref, rb_ref, xo_ref, h_ref, e_ref, gt_ref, acc_scr):
    acc_scr[...] = jnp.dot(mg_ref[...], w_ref[0], preferred_element_type=F32)
    gate1 = mod_ref[0, 0, 2:3, :]
    shift2 = mod_ref[0, 0, 3:4, :]
    scale2 = 1.0 + mod_ref[0, 0, 4:5, :]
    g = g_ref[0]

    def chunk(rows):
        xn = x_ref[rows, :] + gate1 * acc_scr[rows, :]
        xo_ref[rows, :] = xn
        h_ref[rows, :] = (_rms_scale(xn) * g * scale2 + shift2).astype(BF16)

    _for_chunks(x_ref.shape[0], ROW_CHUNK, chunk)
    logits_t = lax.dot_general(rw_ref[...], h_ref[...], _NT, preferred_element_type=F32)
    experts, gates = _route(logits_t, rb_ref[...])
    for k in range(TOP_K):
        e_ref[k:k + 1, :] = experts[k]
        gt_ref[k:k + 1, :] = gates[k]


def _output_projection(merged, x, mod, norm_g, w_out, router_wt, router_b, l, cond_of_block, tm):
    T, D = x.shape
    E = router_wt.shape[0]
    return pl.pallas_call(
        _out_kernel,
        grid=(T // tm,),
        in_specs=[pl.BlockSpec((tm, D), lambda i: (i, 0)),
                  pl.BlockSpec((tm, D), lambda i: (i, 0)),
                  pl.BlockSpec((1, 1, N_MOD, D), lambda i: (l, cond_of_block(i), 0, 0)),
                  pl.BlockSpec((1, 1, D), lambda i: (l, 0, 0)),
                  pl.BlockSpec((1, D, D), lambda i: (l, 0, 0)),
                  pl.BlockSpec((E, D), lambda i: (0, 0)),
                  pl.BlockSpec((E, 1), lambda i: (0, 0))],
        out_specs=[pl.BlockSpec((tm, D), lambda i: (i, 0)),
                   pl.BlockSpec((tm, D), lambda i: (i, 0)),
                   pl.BlockSpec((TOP_K, tm), lambda i: (0, i)),
                   pl.BlockSpec((TOP_K, tm), lambda i: (0, i))],
        out_shape=[jax.ShapeDtypeStruct((T, D), F32),
                   jax.ShapeDtypeStruct((T, D), BF16),
                   jax.ShapeDtypeStruct((TOP_K, T), I32),
                   jax.ShapeDtypeStruct((TOP_K, T), F32)],
        scratch_shapes=[pltpu.VMEM((tm, D), F32)],
        compiler_params=_params(1, 52),
        name="output_projection_norm2_router",
    )(merged, x, mod, norm_g.reshape(-1, 1, D), w_out, router_wt, router_b.reshape(E, 1))


def _moe_kernel(blk_ref, exp_ref, lo_ref, hi_ref, x_ref, wg_ref, wu_ref, wd_ref, o_ref):
    del exp_ref
    w = pl.program_id(0)
    lo, hi = lo_ref[w], hi_ref[w]
    row0 = blk_ref[w] * MOE_BLOCK

    @pl.when(lo == row0)
    def _():
        o_ref[...] = jnp.zeros(o_ref.shape, o_ref.dtype)

    @pl.when(hi > lo)
    def _():
        x = x_ref[...]
        g = jnp.dot(x, wg_ref[0], preferred_element_type=F32)
        u = jnp.dot(x, wu_ref[0], preferred_element_type=F32)
        hid = (g * _sigmoid(g)) * u
        y = jnp.dot(hid.astype(BF16), wd_ref[0], preferred_element_type=F32)
        row = row0 + lax.broadcasted_iota(I32, (MOE_BLOCK, 1), 0)
        o_ref[...] += jnp.where((row >= lo) & (row < hi), y, 0.0)


def _moe_experts(xs, items, w_gate, w_up, w_down, l):
    A, D = xs.shape
    E, DE = w_gate.shape[1], w_gate.shape[3]
    w_gate = w_gate.reshape(-1, D, DE)
    w_up = w_up.reshape(-1, D, DE)
    w_down = w_down.reshape(-1, DE, D)
    n_items = items[0].shape[0]
    grid_spec = pltpu.PrefetchScalarGridSpec(
        num_scalar_prefetch=4,
        grid=(n_items,),
        in_specs=[pl.BlockSpec((MOE_BLOCK, D), lambda w, blk, ex, lo, hi: (blk[w], 0)),
                  pl.BlockSpec((1, D, DE), lambda w, blk, ex, lo, hi: (l * E + ex[w], 0, 0)),
                  pl.BlockSpec((1, D, DE), lambda w, blk, ex, lo, hi: (l * E + ex[w], 0, 0)),
                  pl.BlockSpec((1, DE, D), lambda w, blk, ex, lo, hi: (l * E + ex[w], 0, 0))],
        out_specs=pl.BlockSpec((MOE_BLOCK, D), lambda w, blk, ex, lo, hi: (blk[w], 0)),
    )
    return pl.pallas_call(
        _moe_kernel,
        grid_spec=grid_spec,
        out_shape=jax.ShapeDtypeStruct((A, D), F32),
        compiler_params=_params(1, 48),
        name="moe_experts",
    )(*items, xs, w_gate, w_up, w_down)


def _dispatch_plan(experts, T):
    A = TOP_K * T
    nb = A // MOE_BLOCK
    n_items = nb + N_EXPERTS - 1
    e_flat = experts.reshape(A)
    iota = jnp.arange(A, dtype=I32)
    _, order = lax.sort((e_flat, iota), num_keys=1, is_stable=True)
    _, rank = lax.sort((order, iota), num_keys=1)
    eid = jnp.arange(N_EXPERTS, dtype=I32)
    counts = jnp.sum((e_flat[:, None] == eid[None, :]).astype(I32), axis=0)
    ends = jnp.cumsum(counts)
    starts = ends - counts
    first_blk = starts // MOE_BLOCK
    n_blk = jnp.where(counts > 0, (ends - 1) // MOE_BLOCK - first_blk + 1, 0)
    item_end = jnp.cumsum(n_blk)
    item_start = item_end - n_blk
    w = jnp.arange(n_items, dtype=I32)
    valid = w < item_end[-1]
    ex = jnp.minimum(jnp.sum((w[:, None] >= item_end[None, :]).astype(I32), axis=1), N_EXPERTS - 1)
    pick = lambda tbl: jnp.sum(jnp.where(ex[:, None] == eid[None, :], tbl[None, :], 0), axis=1)
    blk = jnp.where(valid, pick(first_blk) + w - pick(item_start), nb - 1)
    lo = jnp.where(valid, jnp.maximum(pick(starts), blk * MOE_BLOCK), 0)
    hi = jnp.where(valid, jnp.minimum(pick(ends), (blk + 1) * MOE_BLOCK), 0)
    ex = jnp.where(valid, ex, jnp.max(jnp.where(counts > 0, eid, 0)))
    return order % T, rank, (blk.astype(I32), ex.astype(I32), lo.astype(I32), hi.astype(I32))


def _combined(x_ref, y1_ref, y2_ref, g1_ref, g2_ref, gate2, rows):
    y = g1_ref[rows, :] * y1_ref[rows, :] + g2_ref[rows, :] * y2_ref[rows, :]
    return x_ref[rows, :] + gate2 * y


def _combine_kernel(x_ref, y1_ref, y2_ref, g1_ref, g2_ref, mod_ref, modn_ref, gn_ref, xo_ref, h_ref):
    gate2 = mod_ref[0, 0, 5:6, :]
    scale1 = 1.0 + modn_ref[0, 0, 1:2, :]
    shift1 = modn_ref[0, 0, 0:1, :]
    g = gn_ref[0]

    def chunk(rows):
        xn = _combined(x_ref, y1_ref, y2_ref, g1_ref, g2_ref, gate2, rows)
        xo_ref[rows, :] = xn
        h_ref[rows, :] = (_rms_scale(xn) * g * scale1 + shift1).astype(BF16)

    _for_chunks(x_ref.shape[0], ROW_CHUNK, chunk)


def _combine_final_kernel(x_ref, y1_ref, y2_ref, g1_ref, g2_ref, mod_ref, gf_ref, yp_ref, ys_ref, *, ncb):
    i = pl.program_id(0)
    gate2 = mod_ref[0, 0, 5:6, :]
    g = gf_ref[...]

    def run(dst_ref):
        def chunk(rows):
            xn = _combined(x_ref, y1_ref, y2_ref, g1_ref, g2_ref, gate2, rows)
            dst_ref[rows, :] = _rms_scale(xn) * g

        _for_chunks(x_ref.shape[0], ROW_CHUNK, chunk)

    pl.when(i < ncb)(lambda: run(yp_ref))
    pl.when(i >= ncb)(lambda: run(ys_ref))


def _combine(x, y1, y2, g1, g2, mod, norm1_g, l, cond_of_block, tm):
    T, D = x.shape
    row = pl.BlockSpec((tm, D), lambda i: (i, 0))
    col = pl.BlockSpec((tm, 1), lambda i: (i, 0))
    return pl.pallas_call(
        _combine_kernel,
        grid=(T // tm,),
        in_specs=[row, row, row, col, col,
                  pl.BlockSpec((1, 1, N_MOD, D), lambda i: (l, cond_of_block(i), 0, 0)),
                  pl.BlockSpec((1, 1, N_MOD, D), lambda i: (l + 1, cond_of_block(i), 0, 0)),
                  pl.BlockSpec((1, 1, D), lambda i: (l + 1, 0, 0))],
        out_specs=[row, row],
        out_shape=[jax.ShapeDtypeStruct((T, D), F32), jax.ShapeDtypeStruct((T, D), BF16)],
        compiler_params=_params(1, 52),
        name="moe_combine_residual_norm1",
    )(x, y1, y2, g1, g2, mod, mod, norm1_g.reshape(-1, 1, D))


def _combine_final(x, y1, y2, g1, g2, mod, final_g, l, nc, cond_of_block, tm):
    T, D = x.shape
    ncb = nc // tm
    row = pl.BlockSpec((tm, D), lambda i: (i, 0))
    col = pl.BlockSpec((tm, 1), lambda i: (i, 0))
    return pl.pallas_call(
        functools.partial(_combine_final_kernel, ncb=ncb),
        grid=(T // tm,),
        in_specs=[row, row, row, col, col,
                  pl.BlockSpec((1, 1, N_MOD, D), lambda i: (l, cond_of_block(i), 0, 0)),
                  pl.BlockSpec((1, D), lambda i: (0, 0))],
        out_specs=[pl.BlockSpec((tm, D), lambda i: (jnp.minimum(i, ncb - 1), 0)),
                   pl.BlockSpec((tm, D), lambda i: (jnp.maximum(i - ncb, 0), 0))],
        out_shape=[jax.ShapeDtypeStruct((nc, D), F32), jax.ShapeDtypeStruct((T - nc, D), F32)],
        compiler_params=_params(1, 52),
        name="moe_combine_residual_final_norm",
    )(x, y1, y2, g1, g2, mod, final_g.reshape(1, D))


def kernel(x_prompt, x_sample, cache_k, cache_v, c, c_ctx, w_ada, b_ada, norm1_g, w_in, gmlp_g, w_s, b_s, rpb,
           w_proj_a, w_proj_b, w_out, norm2_g, router_w, router_b, w_gate, w_up, w_down, final_g):
    batch, seq, D = x_prompt.shape
    dec_batch, dec_seq, _ = x_sample.shape
    depth = w_in.shape[0]
    heads, dh = cache_k.shape[3], cache_k.shape[4]
    past = cache_k.shape[2]
    nc, nl = batch * seq, dec_batch * dec_seq
    T = nc + nl
    assert heads * dh == D and gmlp_g.shape[1] == D and w_in.shape[2] == 7 * D
    assert 1 + dec_batch <= COND_ROWS
    assert router_w.shape[1] == N_EXPERTS
    assert seq % GMLP_CHUNK == 0 and dec_seq % GMLP_CHUNK == 0 and dec_seq % GRID_W == 0
    assert nc % dec_seq == 0 and (TOP_K * T) % MOE_BLOCK == 0

    tm_big = 1024 if (nc % 1024 == 0 and dec_seq % 1024 == 0) else GMLP_CHUNK
    tm_mid = 512 if (nc % 512 == 0 and dec_seq % 512 == 0) else GMLP_CHUNK

    def cond_of(tm):
        ncb, per = nc // tm, dec_seq // tm
        return lambda i: jnp.where(i < ncb, 0, 1 + (i - ncb) // per)

    cond = jnp.zeros((COND_ROWS, D), F32).at[0].set(c_ctx).at[1:1 + dec_batch].set(c)
    mod = _modulation_all(cond, w_ada, b_ada)

    w_s_b = w_s.astype(BF16)
    w_pa_b = w_proj_a.astype(BF16)
    w_pb_b = w_proj_b.astype(BF16)
    w_out_b = w_out.astype(BF16)
    w_gate_b = w_gate.astype(BF16)
    w_up_b = w_up.astype(BF16)
    w_down_b = w_down.astype(BF16)
    kc_all = cache_k.astype(BF16).reshape(dec_batch, depth, past, D)
    vc_all = cache_v.astype(BF16).reshape(dec_batch, depth, past, D)
    router_wt = router_w.T.astype(BF16)
    u_all, row_mask = _na_tables(rpb, dec_seq // GRID_W)

    x, h = _prologue(x_prompt.reshape(nc, D), x_sample.reshape(nl, D), mod, norm1_g, cond_of(tm_mid), tm_mid)
    new_kv = []
    for l in range(depth):
        proj, kv_ctx = _input_projection(h, w_in, l, nc, tm_big, dh ** -0.5)
        new_kv.append(kv_ctx.reshape(batch, seq, 2, heads, dh))
        ya = _gmlp(proj, gmlp_g, w_s_b, b_s, l, D, tm_mid)
        yb_ctx = _context_attention(proj, batch, seq, D, heads, dh)
        yb_lat = _neighbourhood_attention(proj, kc_all[:, l], vc_all[:, l], u_all, row_mask, l, nc, dec_batch,
                                          dec_seq, D, heads, dh)
        merged = _merge(ya, yb_ctx, yb_lat, proj, w_pa_b, w_pb_b, l, D, tm_big)
        x, h2, experts, gates = _output_projection(merged, x, mod, norm2_g, w_out_b, router_wt, router_b, l,
                                                   cond_of(tm_mid), tm_mid)
        tok_sorted, rank, items = _dispatch_plan(experts, T)
        yexp = _moe_experts(h2[tok_sorted], items, w_gate_b, w_up_b, w_down_b, l)
        y1, y2, g1, g2 = yexp[rank[:T]], yexp[rank[T:]], gates[0][:, None], gates[1][:, None]
        if l + 1 < depth:
            x, h = _combine(x, y1, y2, g1, g2, mod, norm1_g, l, cond_of(tm_mid), tm_mid)
        else:
            y_prompt, y_sample = _combine_final(x, y1, y2, g1, g2, mod, final_g, l, nc, cond_of(tm_mid), tm_mid)

    new_kv = jnp.stack(new_kv, axis=1)
    return (y_prompt.reshape(batch, seq, D), y_sample.reshape(dec_batch, dec_seq, D),
            new_kv[:, :, :, 0], new_kv[:, :, :, 1])
```

```python
import functools

import numpy as np
import jax
import jax.numpy as jnp
from jax import lax
from jax.experimental import pallas as pl
from jax.experimental.pallas import tpu as pltpu

F32 = jnp.float32
BF16 = jnp.bfloat16
I32 = jnp.int32

EPS = 1e-6
N_MOD = 6
GRID_W = 64
WIN_ROWS = 8
WIN_COLS = 16
GMLP_CHUNK = 128
GMLP_GROUPS = 8
N_EXPERTS = 16
N_EXPERT_GROUPS = 4
EXPERTS_PER_GROUP = N_EXPERTS // N_EXPERT_GROUPS
TOP_K = 2
MOE_BLOCK = 256
COND_ROWS = 8
ROW_CHUNK = 64
CAST_CHUNK = 256
NA_QROWS = 4
NA_KROWS = 12
NA_KBLOCK = 256
NA_HEADS_PER_STEP = 4
MIB = 1024 * 1024

_NT = (((1,), (1,)), ((), ()))


def _tile(n, pref):
    return pref if n % pref == 0 else n


def _params(n_axes, vmem_mib):
    return pltpu.CompilerParams(dimension_semantics=("arbitrary",) * n_axes,
                                vmem_limit_bytes=vmem_mib * MIB)


def _for_chunks(n, c, fn):
    c = min(c, n)

    def body(t, carry):
        fn(pl.ds(pl.multiple_of(t * c, c), c))
        return carry

    lax.fori_loop(0, n // c, body, 0)


def _sigmoid(x):
    return 1.0 / (1.0 + jnp.exp(-x))


def _gelu_tanh(x):
    return x * (0.5 * (1.0 + jnp.tanh(0.7978845608028654 * (x + 0.044715 * (x * x * x)))))


def _rms_scale(x):
    return x * lax.rsqrt(jnp.mean(x * x, axis=-1, keepdims=True) + EPS)


def _ada_kernel(c_ref, w_ref, b_ref, o_ref):
    c = c_ref[...]
    s = (c * _sigmoid(c)).astype(BF16)
    o_ref[0] = jnp.dot(s, w_ref[0].astype(BF16), preferred_element_type=F32) + b_ref[0]


def _modulation_all(cond, w_ada, b_ada):
    L, D, N = w_ada.shape
    tn = _tile(N, 1024)
    out = pl.pallas_call(
        _ada_kernel,
        grid=(L, N // tn),
        in_specs=[pl.BlockSpec((COND_ROWS, D), lambda l, j: (0, 0)),
                  pl.BlockSpec((1, D, tn), lambda l, j: (l, 0, j)),
                  pl.BlockSpec((1, 1, tn), lambda l, j: (l, 0, j))],
        out_specs=pl.BlockSpec((1, COND_ROWS, tn), lambda l, j: (l, 0, j)),
        out_shape=jax.ShapeDtypeStruct((L, COND_ROWS, N), F32),
        compiler_params=_params(2, 40),
        name="ada_modulation",
    )(cond, w_ada, b_ada.reshape(L, 1, N))
    return out.reshape(L, COND_ROWS, N_MOD, D)


def _prologue_kernel(xp_ref, xs_ref, mod_ref, g_ref, x_ref, h_ref, *, ncb):
    i = pl.program_id(0)
    g = g_ref[0]
    scale1 = 1.0 + mod_ref[0, 0, 1:2, :]
    shift1 = mod_ref[0, 0, 0:1, :]

    def run(src_ref):
        def chunk(rows):
            x = src_ref[rows, :]
            x_ref[rows, :] = x
            h_ref[rows, :] = (_rms_scale(x) * g * scale1 + shift1).astype(BF16)

        _for_chunks(x_ref.shape[0], ROW_CHUNK, chunk)

    pl.when(i < ncb)(lambda: run(xp_ref))
    pl.when(i >= ncb)(lambda: run(xs_ref))


def _prologue(xp, xs, mod, norm_g, cond_of_block, tm):
    nc, D = xp.shape
    T = nc + xs.shape[0]
    ncb = nc // tm
    row = pl.BlockSpec((tm, D), lambda i: (i, 0))
    return pl.pallas_call(
        functools.partial(_prologue_kernel, ncb=ncb),
        grid=(T // tm,),
        in_specs=[pl.BlockSpec((tm, D), lambda i: (jnp.minimum(i, ncb - 1), 0)),
                  pl.BlockSpec((tm, D), lambda i: (jnp.maximum(i - ncb, 0), 0)),
                  pl.BlockSpec((1, 1, N_MOD, D), lambda i: (0, cond_of_block(i), 0, 0)),
                  pl.BlockSpec((1, 1, D), lambda i: (0, 0, 0))],
        out_specs=[row, row],
        out_shape=[jax.ShapeDtypeStruct((T, D), F32), jax.ShapeDtypeStruct((T, D), BF16)],
        compiler_params=_params(1, 40),
        name="concat_norm1",
    )(xp, xs, mod, norm_g.reshape(-1, 1, D))


def _proj_kernel(h_ref, w_ref, nk_in, nv_in, o_ref, k_ref, v_ref, w_scr, *, ncb, jk0, nk, qscale):
    del nk_in, nv_in
    j = pl.program_id(0)
    i = pl.program_id(1)

    @pl.when(i == 0)
    def _():
        def chunk(rows):
            w_scr[rows, :] = w_ref[0, rows, :].astype(BF16)

        _for_chunks(w_scr.shape[0], CAST_CHUNK, chunk)

    acc = jnp.dot(h_ref[...], w_scr[...], preferred_element_type=F32)
    o_ref[...] = (acc * jnp.where(j < jk0, qscale, 1.0)).astype(BF16)

    def put(dst_ref):
        seq = dst_ref.shape[2]
        for b in range(dst_ref.shape[0]):
            dst_ref[b, 0] = acc[b * seq:(b + 1) * seq, :]

    pl.when((i < ncb) & (j >= jk0) & (j < jk0 + nk))(lambda: put(k_ref))
    pl.when((i < ncb) & (j >= jk0 + nk) & (j < jk0 + 2 * nk))(lambda: put(v_ref))


def _input_projection(h, w_in, new_k, new_v, l, tm, qscale):
    T, D = h.shape
    N = w_in.shape[-1]
    batch, _, seq, _ = new_k.shape
    tn = _tile(D, 1024)
    ncb = batch * seq // tm
    jk0, nk = D // tn, D // tn
    kernel = functools.partial(_proj_kernel, ncb=ncb, jk0=jk0, nk=nk, qscale=qscale)

    def cache_block(j0):
        def index(j, i):
            inside = (j >= j0) & (j < j0 + nk)
            row = jnp.where(j < j0, 0, jnp.where(inside, jnp.minimum(i, ncb - 1), ncb - 1))
            col = jnp.where(j < j0, 0, jnp.where(inside, j - j0, nk - 1))
            return (row, l, 0, col)
        return pl.BlockSpec((tm // seq, 1, seq, tn), index)

    return pl.pallas_call(
        kernel,
        grid=(N // tn, T // tm),
        in_specs=[pl.BlockSpec((tm, D), lambda j, i: (i, 0)),
                  pl.BlockSpec((1, D, tn), lambda j, i: (l, 0, j)),
                  pl.BlockSpec(memory_space=pl.ANY),
                  pl.BlockSpec(memory_space=pl.ANY)],
        out_specs=[pl.BlockSpec((tm, tn), lambda j, i: (i, j)), cache_block(jk0), cache_block(jk0 + nk)],
        out_shape=[jax.ShapeDtypeStruct((T, N), BF16),
                   jax.ShapeDtypeStruct(new_k.shape, F32),
                   jax.ShapeDtypeStruct(new_v.shape, F32)],
        scratch_shapes=[pltpu.VMEM((D, tn), BF16)],
        input_output_aliases={2: 1, 3: 2},
        compiler_params=_params(2, 60),
        name="input_projection",
    )(h, w_in, new_k, new_v)


def _gmlp_kernel(u_ref, gv_ref, g_ref, ws_ref, bs_ref, o_ref):
    gw = u_ref.shape[1] // GMLP_GROUPS
    g = g_ref[0]

    def chunk(rows):
        v = _rms_scale(_gelu_tanh(gv_ref[rows, :].astype(F32))) * g
        v = v.astype(BF16)
        for grp in range(GMLP_GROUPS):
            cols = slice(grp * gw, (grp + 1) * gw)
            s = jnp.dot(ws_ref[grp], v[:, cols], preferred_element_type=F32) + bs_ref[grp]
            u = _gelu_tanh(u_ref[rows, cols].astype(F32))
            o_ref[rows, cols] = (u * s).astype(BF16)

    _for_chunks(u_ref.shape[0], GMLP_CHUNK, chunk)


def _gmlp(proj, gmlp_g, w_s, b_s, l, D, tm):
    T = proj.shape[0]
    gw = D // GMLP_GROUPS
    bs = jnp.broadcast_to(b_s[l][:, :, None], (GMLP_GROUPS, GMLP_CHUNK, gw))
    return pl.pallas_call(
        _gmlp_kernel,
        grid=(T // tm,),
        in_specs=[pl.BlockSpec((tm, D), lambda i: (i, 3)),
                  pl.BlockSpec((tm, D), lambda i: (i, 4)),
                  pl.BlockSpec((1, 1, D), lambda i: (l, 0, 0)),
                  pl.BlockSpec((None, GMLP_GROUPS, GMLP_CHUNK, GMLP_CHUNK), lambda i: (l, 0, 0, 0)),
                  pl.BlockSpec((GMLP_GROUPS, GMLP_CHUNK, gw), lambda i: (0, 0, 0))],
        out_specs=pl.BlockSpec((tm, D), lambda i: (i, 0)),
        out_shape=jax.ShapeDtypeStruct((T, D), BF16),
        compiler_params=_params(1, 40),
        name="gmlp_spatial_gating",
    )(proj, proj, gmlp_g.reshape(-1, 1, D), w_s, bs)


def _ctx_attn_kernel(q_ref, k_ref, v_ref, o_ref, *, heads, dh):
    for h in range(heads):
        cols = slice(h * dh, (h + 1) * dh)
        s = lax.dot_general(q_ref[:, cols], k_ref[:, cols], _NT, preferred_element_type=F32)
        p = jnp.exp(s - jnp.max(s, axis=-1, keepdims=True))
        denom = jnp.sum(p, axis=-1, keepdims=True)
        o = jnp.dot(p.astype(BF16), v_ref[:, cols], preferred_element_type=F32)
        o_ref[:, cols] = (o / denom).astype(BF16)


def _context_attention(proj, batch, seq, D, heads, dh):
    return pl.pallas_call(
        functools.partial(_ctx_attn_kernel, heads=heads, dh=dh),
        grid=(batch,),
        in_specs=[pl.BlockSpec((seq, D), lambda b: (b, 0)),
                  pl.BlockSpec((seq, D), lambda b: (b, 1)),
                  pl.BlockSpec((seq, D), lambda b: (b, 2))],
        out_specs=pl.BlockSpec((seq, D), lambda b: (b, 0)),
        out_shape=jax.ShapeDtypeStruct((batch * seq, D), BF16),
        compiler_params=_params(1, 32),
        name="context_attention",
    )(proj, proj, proj)


def _na_kernel(q_ref, k_ref, v_ref, kc_ref, vc_ref, u_ref, mk_ref, o_ref, vt_scr, vct_scr, *, hps, dh, rows):
    m = pl.program_id(2)
    nblk = rows // NA_QROWS
    win = NA_KROWS * GRID_W
    nkb = win // NA_KBLOCK

    @pl.when(m == 0)
    def _():
        for kb in range(v_ref.shape[0] // NA_KBLOCK):
            vt_scr[kb] = v_ref[kb * NA_KBLOCK:(kb + 1) * NA_KBLOCK, :].astype(F32).T.astype(BF16)
        vct_scr[...] = vc_ref[0].astype(F32).T.astype(BF16)

    r0 = m * NA_QROWS
    ws = jnp.clip(r0 - WIN_ROWS // 2, 0, rows - NA_KROWS)
    krows = pl.ds(pl.multiple_of(ws * GRID_W, NA_KBLOCK), win)
    urows = pl.ds(pl.multiple_of((ws - r0 + WIN_ROWS) * GRID_W, NA_KBLOCK), win)
    kb0 = ws // (NA_KBLOCK // GRID_W)
    row_mask = mk_ref[jnp.where(m == 0, 0, jnp.where(m == nblk - 1, 2, 1))]
    for h in range(hps):
        cols = slice(h * dh, (h + 1) * dh)
        q = q_ref[:, cols]
        s_loc = (lax.dot_general(k_ref[krows, cols], q, _NT, preferred_element_type=F32)
                 + u_ref[0, h, urows, :] + row_mask)
        s_ctx = lax.dot_general(kc_ref[0, :, cols], q, _NT, preferred_element_type=F32)
        mx = jnp.maximum(jnp.max(s_loc, axis=0, keepdims=True), jnp.max(s_ctx, axis=0, keepdims=True))
        p_loc = jnp.exp(s_loc - mx)
        p_ctx = jnp.exp(s_ctx - mx)
        denom = jnp.sum(p_loc, axis=0, keepdims=True) + jnp.sum(p_ctx, axis=0, keepdims=True)
        p_loc = p_loc.astype(BF16)
        ot = jnp.dot(vct_scr[cols, :], p_ctx.astype(BF16), preferred_element_type=F32)
        for t in range(nkb):
            ot = ot + jnp.dot(vt_scr[kb0 + t, cols, :], p_loc[t * NA_KBLOCK:(t + 1) * NA_KBLOCK, :],
                              preferred_element_type=F32)
        o_ref[:, cols] = (ot / denom).T.astype(BF16)


def _na_tables(rpb, rows):
    L, H = rpb.shape[:2]
    kh = WIN_ROWS
    col = np.arange(GRID_W)
    cs = np.clip(col - WIN_COLS // 2, 0, GRID_W - WIN_COLS)
    col_valid = (col[:, None] >= cs[None, :]) & (col[:, None] < cs[None, :] + WIN_COLS)
    n_rho = NA_KROWS + WIN_ROWS
    n_dr = 2 * WIN_ROWS - 1
    j = np.arange(NA_QROWS)
    flipped = jnp.pad(jnp.flip(rpb.astype(F32), axis=-1), ((0, 0), (0, 0), (0, 0), (GRID_W, GRID_W)))
    t1 = jnp.stack([flipped[..., GRID_W + WIN_COLS - 1 - kc:2 * GRID_W + WIN_COLS - 1 - kc] for kc in range(GRID_W)],
                   axis=3)
    t1 = jnp.where(col_valid, t1, -jnp.inf)
    lanes = []
    for jj in range(NA_QROWS):
        dr = np.clip(np.arange(n_rho) - WIN_ROWS - jj + WIN_ROWS - 1, 0, n_dr - 1)
        lo, hi = int((dr == 0).sum()) - 1, int((dr == n_dr - 1).sum()) - 1
        assert lo + n_dr + hi == n_rho
        lanes.append(jnp.concatenate([t1[:, :, :1]] * lo + [t1] + [t1[:, :, -1:]] * hi, axis=2))
    u = jnp.concatenate(lanes, axis=-1).reshape(L, H, n_rho * GRID_W, NA_QROWS * GRID_W)

    def band(r0, ws):
        rs = np.clip(r0 + j - kh // 2, 0, rows - kh)
        key_row = ws + np.arange(NA_KROWS)
        ok = (key_row[:, None] >= rs[None, :]) & (key_row[:, None] < rs[None, :] + kh)
        mk = np.where(ok, 0.0, -np.inf).astype(np.float32)
        return np.broadcast_to(mk[:, None, :, None], (NA_KROWS, GRID_W, NA_QROWS, GRID_W)).reshape(
            NA_KROWS * GRID_W, NA_QROWS * GRID_W)

    row_mask = np.stack([band(0, 0), band(NA_QROWS, 0), band(rows - NA_QROWS, rows - NA_KROWS)])
    return u, jnp.asarray(row_mask)


def _neighbourhood_attention(proj, kc, vc, u, row_mask, l, nc, dec_batch, dec_seq, D, heads, dh):
    rows = dec_seq // GRID_W
    assert rows % NA_QROWS == 0 and rows >= NA_KROWS
    hps = min(NA_HEADS_PER_STEP, heads)
    qn = NA_QROWS * GRID_W
    cw = hps * dh
    n_hg = heads // hps
    nblk = rows // NA_QROWS
    q0 = nc // qn
    k0 = nc // dec_seq
    ncols = D // cw
    past = kc.shape[1]
    kernel = functools.partial(_na_kernel, hps=hps, dh=dh, rows=rows)
    return pl.pallas_call(
        kernel,
        grid=(dec_batch, n_hg, nblk),
        in_specs=[pl.BlockSpec((qn, cw), lambda b, g, m: (q0 + b * nblk + m, g)),
                  pl.BlockSpec((dec_seq, cw), lambda b, g, m: (k0 + b, ncols + g)),
                  pl.BlockSpec((dec_seq, cw), lambda b, g, m: (k0 + b, 2 * ncols + g)),
                  pl.BlockSpec((1, past, cw), lambda b, g, m: (b, 0, g)),
                  pl.BlockSpec((1, past, cw), lambda b, g, m: (b, 0, g)),
                  pl.BlockSpec((1, hps) + u.shape[2:], lambda b, g, m: (l, g, 0, 0)),
                  pl.BlockSpec(row_mask.shape, lambda b, g, m: (0, 0, 0))],
        out_specs=pl.BlockSpec((qn, cw), lambda b, g, m: (b * nblk + m, g)),
        out_shape=jax.ShapeDtypeStruct((dec_batch * dec_seq, D), BF16),
        scratch_shapes=[pltpu.VMEM((dec_seq // NA_KBLOCK, cw, NA_KBLOCK), BF16),
                        pltpu.VMEM((cw, past), BF16)],
        compiler_params=_params(3, 52),
        name="neighbourhood_attention",
    )(proj, proj, proj, kc, vc, u, row_mask)


def _merge_kernel(ya_ref, ybc_ref, ybl_ref, ga_ref, gb_ref, wa_ref, wb_ref, o_ref, *, ncb):
    yb = jnp.where(pl.program_id(0) < ncb, ybc_ref[...], ybl_ref[...])
    a = jnp.dot(ya_ref[...], wa_ref[0], preferred_element_type=F32)
    b = jnp.dot(yb, wb_ref[0], preferred_element_type=F32)
    o = _sigmoid(ga_ref[...].astype(F32)) * a + _sigmoid(gb_ref[...].astype(F32)) * b
    o_ref[...] = o.astype(BF16)


def _merge(ya, yb_ctx, yb_lat, proj, w_pa, w_pb, l, D, tm):
    T = ya.shape[0]
    tn = _tile(D, 512)
    nj = D // tn
    ncb = yb_ctx.shape[0] // tm
    return pl.pallas_call(
        functools.partial(_merge_kernel, ncb=ncb),
        grid=(T // tm, nj),
        in_specs=[pl.BlockSpec((tm, D), lambda i, j: (i, 0)),
                  pl.BlockSpec((tm, D), lambda i, j: (jnp.minimum(i, ncb - 1), 0)),
                  pl.BlockSpec((tm, D), lambda i, j: (jnp.maximum(i - ncb, 0), 0)),
                  pl.BlockSpec((tm, tn), lambda i, j: (i, 5 * nj + j)),
                  pl.BlockSpec((tm, tn), lambda i, j: (i, 6 * nj + j)),
                  pl.BlockSpec((1, D, tn), lambda i, j: (l, 0, j)),
                  pl.BlockSpec((1, D, tn), lambda i, j: (l, 0, j))],
        out_specs=pl.BlockSpec((tm, tn), lambda i, j: (i, j)),
        out_shape=jax.ShapeDtypeStruct((T, D), BF16),
        compiler_params=_params(2, 56),
        name="gated_branch_merge",
    )(ya, yb_ctx, yb_lat, proj, proj, w_pa, w_pb)


def _argmax_first(vals):
    best = vals[0]
    idx = jnp.zeros(best.shape, I32)
    for k in range(1, len(vals)):
        upd = vals[k] > best
        idx = jnp.where(upd, k, idx)
        best = jnp.where(upd, vals[k], best)
    return idx


def _select(idx, vals):
    out = vals[-1]
    for k in range(len(vals) - 2, -1, -1):
        out = jnp.where(idx == k, vals[k], out)
    return out


def _route(logits_t, router_b):
    m = jnp.max(logits_t, axis=0, keepdims=True)
    ex = jnp.exp(logits_t - m)
    probs = ex / jnp.sum(ex, axis=0, keepdims=True)
    sel = probs + router_b
    sel_rows = [sel[e:e + 1, :] for e in range(N_EXPERTS)]
    prob_rows = [probs[e:e + 1, :] for e in range(N_EXPERTS)]
    scores = []
    for g in range(N_EXPERT_GROUPS):
        a, b, c, d = sel_rows[g * EXPERTS_PER_GROUP:(g + 1) * EXPERTS_PER_GROUP]
        hi1, lo1 = jnp.maximum(a, b), jnp.minimum(a, b)
        hi2, lo2 = jnp.maximum(c, d), jnp.minimum(c, d)
        top1 = jnp.maximum(hi1, hi2)
        top2 = jnp.maximum(jnp.minimum(hi1, hi2), jnp.where(hi1 >= hi2, lo1, lo2))
        scores.append(top1 + top2)
    best_g = _argmax_first(scores)
    in_sel = [_select(best_g, [sel_rows[g * EXPERTS_PER_GROUP + k] for g in range(N_EXPERT_GROUPS)])
              for k in range(EXPERTS_PER_GROUP)]
    in_prob = [_select(best_g, [prob_rows[g * EXPERTS_PER_GROUP + k] for g in range(N_EXPERT_GROUPS)])
               for k in range(EXPERTS_PER_GROUP)]
    i1 = _argmax_first(in_sel)
    i2 = _argmax_first([jnp.where(i1 == k, -jnp.inf, in_sel[k]) for k in range(EXPERTS_PER_GROUP)])
    w1 = _select(i1, in_prob)
    w2 = _select(i2, in_prob)
    tot = w1 + w2
    base = best_g * EXPERTS_PER_GROUP
    return (base + i1, base + i2), (w1 / tot, w2 / tot)


def _out_kernel(mg_ref, x_ref, mod_ref, g_ref, w_ref, rw_ref, rb_ref, xo_ref, h_ref, e_ref, gt_ref, acc_scr):
    acc_scr[...] = jnp.dot(mg_ref[...], w_ref[0], preferred_element_type=F32)
    gate1 = mod_ref[0, 0, 2:3, :]
    shift2 = mod_ref[0, 0, 3:4, :]
    scale2 = 1.0 + mod_ref[0, 0, 4:5, :]
    g = g_ref[0]

    def chunk(rows):
        xn = x_ref[rows, :] + gate1 * acc_scr[rows, :]
        xo_ref[rows, :] = xn
        h_ref[rows, :] = (_rms_scale(xn) * g * scale2 + shift2).astype(BF16)

    _for_chunks(x_ref.shape[0], ROW_CHUNK, chunk)
    logits_t = lax.dot_general(rw_ref[...], h_ref[...], _NT, preferred_element_type=F32)
    experts, gates = _route(logits_t, rb_ref[...])
    for k in range(TOP_K):
        e_ref[k:k + 1, :] = experts[k]
        gt_ref[k:k + 1, :] = gates[k]


def _output_projection(merged, x, mod, norm_g, w_out, router_wt, router_b, l, cond_of_block, tm):
    T, D = x.shape
    E = router_wt.shape[0]
    return pl.pallas_call(
        _out_kernel,
        grid=(T // tm,),
        in_specs=[pl.BlockSpec((tm, D), lambda i: (i, 0)),
                  pl.BlockSpec((tm, D), lambda i: (i, 0)),
                  pl.BlockSpec((1, 1, N_MOD, D), lambda i: (l, cond_of_block(i), 0, 0)),
                  pl.BlockSpec((1, 1, D), lambda i: (l, 0, 0)),
                  pl.BlockSpec((1, D, D), lambda i: (l, 0, 0)),
                  pl.BlockSpec((E, D), lambda i: (0, 0)),
                  pl.BlockSpec((E, 1), lambda i: (0, 0))],
        out_specs=[pl.BlockSpec((tm, D), lambda i: (i, 0)),
                   pl.BlockSpec((tm, D), lambda i: (i, 0)),
                   pl.BlockSpec((TOP_K, tm), lambda i: (0, i)),
                   pl.BlockSpec((TOP_K, tm), lambda i: (0, i))],
        out_shape=[jax.ShapeDtypeStruct((T, D), F32),
                   jax.ShapeDtypeStruct((T, D), BF16),
                   jax.ShapeDtypeStruct((TOP_K, T), I32),
                   jax.ShapeDtypeStruct((TOP_K, T), F32)],
        scratch_shapes=[pltpu.VMEM((tm, D), F32)],
        compiler_params=_params(1, 52),
        name="output_projection_norm2_router",
    )(merged, x, mod, norm_g.reshape(-1, 1, D), w_out, router_wt, router_b.reshape(E, 1))


def _moe_kernel(blk_ref, exp_ref, lo_ref, hi_ref, x_ref, wg_ref, wu_ref, wd_ref, o_ref):
    del exp_ref
    w = pl.program_id(0)
    lo, hi = lo_ref[w], hi_ref[w]
    row0 = blk_ref[w] * MOE_BLOCK

    @pl.when(lo == row0)
    def _():
        o_ref[...] = jnp.zeros(o_ref.shape, o_ref.dtype)

    @pl.when(hi > lo)
    def _():
        x = x_ref[...]
        g = jnp.dot(x, wg_ref[0], preferred_element_type=F32)
        u = jnp.dot(x, wu_ref[0], preferred_element_type=F32)
        hid = (g * _sigmoid(g)) * u
        y = jnp.dot(hid.astype(BF16), wd_ref[0], preferred_element_type=F32)
        row = row0 + lax.broadcasted_iota(I32, (MOE_BLOCK, 1), 0)
        o_ref[...] += jnp.where((row >= lo) & (row < hi), y, 0.0).astype(o_ref.dtype)


def _moe_experts(xs, items, w_gate, w_up, w_down, l):
    A, D = xs.shape
    E, DE = w_gate.shape[1], w_gate.shape[3]
    w_gate = w_gate.reshape(-1, D, DE)
    w_up = w_up.reshape(-1, D, DE)
    w_down = w_down.reshape(-1, DE, D)
    n_items = items[0].shape[0]
    grid_spec = pltpu.PrefetchScalarGridSpec(
        num_scalar_prefetch=4,
        grid=(n_items,),
        in_specs=[pl.BlockSpec((MOE_BLOCK, D), lambda w, blk, ex, lo, hi: (blk[w], 0)),
                  pl.BlockSpec((1, D, DE), lambda w, blk, ex, lo, hi: (l * E + ex[w], 0, 0)),
                  pl.BlockSpec((1, D, DE), lambda w, blk, ex, lo, hi: (l * E + ex[w], 0, 0)),
                  pl.BlockSpec((1, DE, D), lambda w, blk, ex, lo, hi: (l * E + ex[w], 0, 0))],
        out_specs=pl.BlockSpec((MOE_BLOCK, D), lambda w, blk, ex, lo, hi: (blk[w], 0)),
    )
    return pl.pallas_call(
        _moe_kernel,
        grid_spec=grid_spec,
        out_shape=jax.ShapeDtypeStruct((A, D), BF16),
        compiler_params=_params(1, 48),
        name="moe_experts",
    )(*items, xs, w_gate, w_up, w_down)


def _dispatch_plan(experts, T):
    A = TOP_K * T
    nb = A // MOE_BLOCK
    n_items = nb + N_EXPERTS - 1
    e_flat = experts.reshape(A)
    iota = jnp.arange(A, dtype=I32)
    _, order = lax.sort((e_flat, iota), num_keys=1, is_stable=True)
    _, rank = lax.sort((order, iota), num_keys=1)
    eid = jnp.arange(N_EXPERTS, dtype=I32)
    counts = jnp.sum((e_flat[:, None] == eid[None, :]).astype(I32), axis=0)
    ends = jnp.cumsum(counts)
    starts = ends - counts
    first_blk = starts // MOE_BLOCK
    n_blk = jnp.where(counts > 0, (ends - 1) // MOE_BLOCK - first_blk + 1, 0)
    item_end = jnp.cumsum(n_blk)
    item_start = item_end - n_blk
    w = jnp.arange(n_items, dtype=I32)
    valid = w < item_end[-1]
    ex = jnp.minimum(jnp.sum((w[:, None] >= item_end[None, :]).astype(I32), axis=1), N_EXPERTS - 1)
    pick = lambda tbl: jnp.sum(jnp.where(ex[:, None] == eid[None, :], tbl[None, :], 0), axis=1)
    blk = jnp.where(valid, pick(first_blk) + w - pick(item_start), nb - 1)
    lo = jnp.where(valid, jnp.maximum(pick(starts), blk * MOE_BLOCK), 0)
    hi = jnp.where(valid, jnp.minimum(pick(ends), (blk + 1) * MOE_BLOCK), 0)
    ex = jnp.where(valid, ex, jnp.max(jnp.where(counts > 0, eid, 0)))
    return order % T, rank, (blk.astype(I32), ex.astype(I32), lo.astype(I32), hi.astype(I32))


def _combined(x_ref, y1_ref, y2_ref, g1_ref, g2_ref, gate2, rows):
    y = g1_ref[rows, :] * y1_ref[rows, :].astype(F32) + g2_ref[rows, :] * y2_ref[rows, :].astype(F32)
    return x_ref[rows, :] + gate2 * y


def _combine_kernel(x_ref, y1_ref, y2_ref, g1_ref, g2_ref, mod_ref, modn_ref, gn_ref, xo_ref, h_ref):
    gate2 = mod_ref[0, 0, 5:6, :]
    scale1 = 1.0 + modn_ref[0, 0, 1:2, :]
    shift1 = modn_ref[0, 0, 0:1, :]
    g = gn_ref[0]

    def chunk(rows):
        xn = _combined(x_ref, y1_ref, y2_ref, g1_ref, g2_ref, gate2, rows)
        xo_ref[rows, :] = xn
        h_ref[rows, :] = (_rms_scale(xn) * g * scale1 + shift1).astype(BF16)

    _for_chunks(x_ref.shape[0], ROW_CHUNK, chunk)


def _combine_final_kernel(x_ref, y1_ref, y2_ref, g1_ref, g2_ref, mod_ref, gf_ref, yp_ref, ys_ref, *, ncb):
    i = pl.program_id(0)
    gate2 = mod_ref[0, 0, 5:6, :]
    g = gf_ref[...]

    def run(dst_ref):
        def chunk(rows):
            xn = _combined(x_ref, y1_ref, y2_ref, g1_ref, g2_ref, gate2, rows)
            dst_ref[rows, :] = _rms_scale(xn) * g

        _for_chunks(x_ref.shape[0], ROW_CHUNK, chunk)

    pl.when(i < ncb)(lambda: run(yp_ref))
    pl.when(i >= ncb)(lambda: run(ys_ref))


def _combine(x, y1, y2, g1, g2, mod, norm1_g, l, cond_of_block, tm):
    T, D = x.shape
    row = pl.BlockSpec((tm, D), lambda i: (i, 0))
    col = pl.BlockSpec((tm, 1), lambda i: (i, 0))
    return pl.pallas_call(
        _combine_kernel,
        grid=(T // tm,),
        in_specs=[row, row, row, col, col,
                  pl.BlockSpec((1, 1, N_MOD, D), lambda i: (l, cond_of_block(i), 0, 0)),
                  pl.BlockSpec((1, 1, N_MOD, D), lambda i: (l + 1, cond_of_block(i), 0, 0)),
                  pl.BlockSpec((1, 1, D), lambda i: (l + 1, 0, 0))],
        out_specs=[row, row],
        out_shape=[jax.ShapeDtypeStruct((T, D), F32), jax.ShapeDtypeStruct((T, D), BF16)],
        compiler_params=_params(1, 52),
        name="moe_combine_residual_norm1",
    )(x, y1, y2, g1, g2, mod, mod, norm1_g.reshape(-1, 1, D))


def _combine_final(x, y1, y2, g1, g2, mod, final_g, l, nc, cond_of_block, tm):
    T, D = x.shape
    ncb = nc // tm
    row = pl.BlockSpec((tm, D), lambda i: (i, 0))
    col = pl.BlockSpec((tm, 1), lambda i: (i, 0))
    return pl.pallas_call(
        functools.partial(_combine_final_kernel, ncb=ncb),
        grid=(T // tm,),
        in_specs=[row, row, row, col, col,
                  pl.BlockSpec((1, 1, N_MOD, D), lambda i: (l, cond_of_block(i), 0, 0)),
                  pl.BlockSpec((1, D), lambda i: (0, 0))],
        out_specs=[pl.BlockSpec((tm, D), lambda i: (jnp.minimum(i, ncb - 1), 0)),
                   pl.BlockSpec((tm, D), lambda i: (jnp.maximum(i - ncb, 0), 0))],
        out_shape=[jax.ShapeDtypeStruct((nc, D), F32), jax.ShapeDtypeStruct((T - nc, D), F32)],
        compiler_params=_params(1, 52),
        name="moe_combine_residual_final_norm",
    )(x, y1, y2, g1, g2, mod, final_g.reshape(1, D))


def kernel(x_prompt, x_sample, cache_k, cache_v, c, c_ctx, w_ada, b_ada, norm1_g, w_in, gmlp_g, w_s, b_s, rpb,
           w_proj_a, w_proj_b, w_out, norm2_g, router_w, router_b, w_gate, w_up, w_down, final_g):
    batch, seq, D = x_prompt.shape
    dec_batch, dec_seq, _ = x_sample.shape
    depth = w_in.shape[0]
    heads, dh = cache_k.shape[3], cache_k.shape[4]
    past = cache_k.shape[2]
    nc, nl = batch * seq, dec_batch * dec_seq
    T = nc + nl
    assert heads * dh == D and gmlp_g.shape[1] == D and w_in.shape[2] == 7 * D
    assert 1 + dec_batch <= COND_ROWS
    assert router_w.shape[1] == N_EXPERTS
    assert seq % GMLP_CHUNK == 0 and dec_seq % GMLP_CHUNK == 0 and dec_seq % GRID_W == 0
    assert nc % dec_seq == 0 and (TOP_K * T) % MOE_BLOCK == 0

    tm_big = 1024 if (nc % 1024 == 0 and dec_seq % 1024 == 0) else GMLP_CHUNK
    tm_mid = 512 if (nc % 512 == 0 and dec_seq % 512 == 0) else GMLP_CHUNK

    def cond_of(tm):
        ncb, per = nc // tm, dec_seq // tm
        return lambda i: jnp.where(i < ncb, 0, 1 + (i - ncb) // per)

    cond = jnp.zeros((COND_ROWS, D), F32).at[0].set(c_ctx).at[1:1 + dec_batch].set(c)
    mod = _modulation_all(cond, w_ada, b_ada)

    w_s_b = w_s.astype(BF16)
    w_pa_b = w_proj_a.astype(BF16)
    w_pb_b = w_proj_b.astype(BF16)
    w_out_b = w_out.astype(BF16)
    w_gate_b = w_gate.astype(BF16)
    w_up_b = w_up.astype(BF16)
    w_down_b = w_down.astype(BF16)
    kc_all = cache_k.astype(BF16).reshape(dec_batch, depth, past, D)
    vc_all = cache_v.astype(BF16).reshape(dec_batch, depth, past, D)
    router_wt = router_w.T.astype(BF16)
    u_all, row_mask = _na_tables(rpb, dec_seq // GRID_W)

    x, h = _prologue(x_prompt.reshape(nc, D), x_sample.reshape(nl, D), mod, norm1_g, cond_of(tm_mid), tm_mid)
    new_k = jnp.zeros((batch, depth, seq, D), F32)
    new_v = jnp.zeros((batch, depth, seq, D), F32)
    for l in range(depth):
        proj, new_k, new_v = _input_projection(h, w_in, new_k, new_v, l, tm_big, dh ** -0.5)
        ya = _gmlp(proj, gmlp_g, w_s_b, b_s, l, D, tm_mid)
        yb_ctx = _context_attention(proj, batch, seq, D, heads, dh)
        yb_lat = _neighbourhood_attention(proj, kc_all[:, l], vc_all[:, l], u_all, row_mask, l, nc, dec_batch,
                                          dec_seq, D, heads, dh)
        merged = _merge(ya, yb_ctx, yb_lat, proj, w_pa_b, w_pb_b, l, D, tm_big)
        x, h2, experts, gates = _output_projection(merged, x, mod, norm2_g, w_out_b, router_wt, router_b, l,
                                                   cond_of(tm_mid), tm_mid)
        tok_sorted, rank, items = _dispatch_plan(experts, T)
        yexp = _moe_experts(h2[tok_sorted], items, w_gate_b, w_up_b, w_down_b, l)
        y1, y2, g1, g2 = yexp[rank[:T]], yexp[rank[T:]], gates[0][:, None], gates[1][:, None]
        if l + 1 < depth:
            x, h = _combine(x, y1, y2, g1, g2, mod, norm1_g, l, cond_of(tm_mid), tm_mid)
        else:
            y_prompt, y_sample = _combine_final(x, y1, y2, g1, g2, mod, final_g, l, nc, cond_of(tm_mid), tm_mid)

    return (y_prompt.reshape(batch, seq, D), y_sample.reshape(dec_batch, dec_seq, D),
            new_k.reshape(batch, depth, seq, heads, dh), new_v.reshape(batch, depth, seq, heads, dh))
```

```python
import functools

import numpy as np
import jax
import jax.numpy as jnp
from jax import lax
from jax.experimental import pallas as pl
from jax.experimental.pallas import tpu as pltpu

F32 = jnp.float32
BF16 = jnp.bfloat16
I32 = jnp.int32

EPS = 1e-6
N_MOD = 6
GRID_W = 64
WIN_ROWS = 8
WIN_COLS = 16
GMLP_CHUNK = 128
GMLP_GROUPS = 8
N_EXPERTS = 16
N_EXPERT_GROUPS = 4
EXPERTS_PER_GROUP = N_EXPERTS // N_EXPERT_GROUPS
TOP_K = 2
MOE_BLOCK = 256
COND_ROWS = 8
ROW_CHUNK = 64
CAST_CHUNK = 256
NA_QROWS = 4
NA_KROWS = 12
NA_KBLOCK = 256
NA_HEADS_PER_STEP = 4
MIB = 1024 * 1024

_NT = (((1,), (1,)), ((), ()))


def _tile(n, pref):
    return pref if n % pref == 0 else n


def _params(n_axes, vmem_mib):
    return pltpu.CompilerParams(dimension_semantics=("arbitrary",) * n_axes,
                                vmem_limit_bytes=vmem_mib * MIB)


def _for_chunks(n, c, fn):
    c = min(c, n)

    def body(t, carry):
        fn(pl.ds(pl.multiple_of(t * c, c), c))
        return carry

    lax.fori_loop(0, n // c, body, 0)


def _sigmoid(x):
    return 1.0 / (1.0 + jnp.exp(-x))


def _gelu_tanh(x):
    return x * (0.5 * (1.0 + jnp.tanh(0.7978845608028654 * (x + 0.044715 * (x * x * x)))))


def _rms_scale(x):
    return x * lax.rsqrt(jnp.mean(x * x, axis=-1, keepdims=True) + EPS)


def _ada_kernel(c_ref, w_ref, b_ref, o_ref):
    c = c_ref[...]
    s = (c * _sigmoid(c)).astype(BF16)
    o_ref[0] = jnp.dot(s, w_ref[0].astype(BF16), preferred_element_type=F32) + b_ref[0]


def _modulation_all(cond, w_ada, b_ada):
    L, D, N = w_ada.shape
    tn = _tile(N, 1024)
    out = pl.pallas_call(
        _ada_kernel,
        grid=(L, N // tn),
        in_specs=[pl.BlockSpec((COND_ROWS, D), lambda l, j: (0, 0)),
                  pl.BlockSpec((1, D, tn), lambda l, j: (l, 0, j)),
                  pl.BlockSpec((1, 1, tn), lambda l, j: (l, 0, j))],
        out_specs=pl.BlockSpec((1, COND_ROWS, tn), lambda l, j: (l, 0, j)),
        out_shape=jax.ShapeDtypeStruct((L, COND_ROWS, N), F32),
        compiler_params=_params(2, 40),
        name="ada_modulation",
    )(cond, w_ada, b_ada.reshape(L, 1, N))
    return out.reshape(L, COND_ROWS, N_MOD, D)


def _prologue_kernel(xp_ref, xs_ref, mod_ref, g_ref, x_ref, h_ref, *, ncb):
    i = pl.program_id(0)
    g = g_ref[0]
    scale1 = 1.0 + mod_ref[0, 0, 1:2, :]
    shift1 = mod_ref[0, 0, 0:1, :]

    def run(src_ref):
        def chunk(rows):
            x = src_ref[rows, :]
            x_ref[rows, :] = x
            h_ref[rows, :] = (_rms_scale(x) * g * scale1 + shift1).astype(BF16)

        _for_chunks(x_ref.shape[0], ROW_CHUNK, chunk)

    pl.when(i < ncb)(lambda: run(xp_ref))
    pl.when(i >= ncb)(lambda: run(xs_ref))


def _prologue(xp, xs, mod, norm_g, cond_of_block, tm):
    nc, D = xp.shape
    T = nc + xs.shape[0]
    ncb = nc // tm
    row = pl.BlockSpec((tm, D), lambda i: (i, 0))
    return pl.pallas_call(
        functools.partial(_prologue_kernel, ncb=ncb),
        grid=(T // tm,),
        in_specs=[pl.BlockSpec((tm, D), lambda i: (jnp.minimum(i, ncb - 1), 0)),
                  pl.BlockSpec((tm, D), lambda i: (jnp.maximum(i - ncb, 0), 0)),
                  pl.BlockSpec((1, 1, N_MOD, D), lambda i: (0, cond_of_block(i), 0, 0)),
                  pl.BlockSpec((1, 1, D), lambda i: (0, 0, 0))],
        out_specs=[row, row],
        out_shape=[jax.ShapeDtypeStruct((T, D), F32), jax.ShapeDtypeStruct((T, D), BF16)],
        compiler_params=_params(1, 40),
        name="concat_norm1",
    )(xp, xs, mod, norm_g.reshape(-1, 1, D))


def _proj_kernel(h_ref, w_ref, nk_in, nv_in, o_ref, k_ref, v_ref, w_scr, *, ncb, jk0, nk, qscale):
    del nk_in, nv_in
    j = pl.program_id(0)
    i = pl.program_id(1)

    @pl.when(i == 0)
    def _():
        def chunk(rows):
            w_scr[rows, :] = w_ref[0, rows, :].astype(BF16)

        _for_chunks(w_scr.shape[0], CAST_CHUNK, chunk)

    acc = jnp.dot(h_ref[...], w_scr[...], preferred_element_type=F32)
    o_ref[...] = (acc * jnp.where(j < jk0, qscale, 1.0)).astype(BF16)

    def put(dst_ref):
        seq = dst_ref.shape[2]
        for b in range(dst_ref.shape[0]):
            dst_ref[b, 0] = acc[b * seq:(b + 1) * seq, :]

    pl.when((i < ncb) & (j >= jk0) & (j < jk0 + nk))(lambda: put(k_ref))
    pl.when((i < ncb) & (j >= jk0 + nk) & (j < jk0 + 2 * nk))(lambda: put(v_ref))


def _input_projection(h, w_in, new_k, new_v, l, tm, qscale):
    T, D = h.shape
    N = w_in.shape[-1]
    batch, _, seq, _ = new_k.shape
    tn = _tile(D, 1024)
    ncb = batch * seq // tm
    jk0, nk = D // tn, D // tn
    kernel = functools.partial(_proj_kernel, ncb=ncb, jk0=jk0, nk=nk, qscale=qscale)

    def cache_block(j0):
        def index(j, i):
            inside = (j >= j0) & (j < j0 + nk)
            row = jnp.where(j < j0, 0, jnp.where(inside, jnp.minimum(i, ncb - 1), ncb - 1))
            col = jnp.where(j < j0, 0, jnp.where(inside, j - j0, nk - 1))
            return (row, l, 0, col)
        return pl.BlockSpec((tm // seq, 1, seq, tn), index)

    return pl.pallas_call(
        kernel,
        grid=(N // tn, T // tm),
        in_specs=[pl.BlockSpec((tm, D), lambda j, i: (i, 0)),
                  pl.BlockSpec((1, D, tn), lambda j, i: (l, 0, j)),
                  pl.BlockSpec(memory_space=pl.ANY),
                  pl.BlockSpec(memory_space=pl.ANY)],
        out_specs=[pl.BlockSpec((tm, tn), lambda j, i: (i, j)), cache_block(jk0), cache_block(jk0 + nk)],
        out_shape=[jax.ShapeDtypeStruct((T, N), BF16),
                   jax.ShapeDtypeStruct(new_k.shape, F32),
                   jax.ShapeDtypeStruct(new_v.shape, F32)],
        scratch_shapes=[pltpu.VMEM((D, tn), BF16)],
        input_output_aliases={2: 1, 3: 2},
        compiler_params=_params(2, 60),
        name="input_projection",
    )(h, w_in, new_k, new_v)


def _gmlp_kernel(u_ref, gv_ref, g_ref, ws_ref, bs_ref, o_ref):
    gw = u_ref.shape[1] // GMLP_GROUPS
    g = g_ref[0]

    def chunk(rows):
        v = _rms_scale(_gelu_tanh(gv_ref[rows, :].astype(F32))) * g
        v = v.astype(BF16)
        for grp in range(GMLP_GROUPS):
            cols = slice(grp * gw, (grp + 1) * gw)
            s = jnp.dot(ws_ref[grp], v[:, cols], preferred_element_type=F32) + bs_ref[grp]
            u = _gelu_tanh(u_ref[rows, cols].astype(F32))
            o_ref[rows, cols] = (u * s).astype(BF16)

    _for_chunks(u_ref.shape[0], GMLP_CHUNK, chunk)


def _gmlp(proj, gmlp_g, w_s, b_s, l, D, tm):
    T = proj.shape[0]
    gw = D // GMLP_GROUPS
    bs = jnp.broadcast_to(b_s[l][:, :, None], (GMLP_GROUPS, GMLP_CHUNK, gw))
    return pl.pallas_call(
        _gmlp_kernel,
        grid=(T // tm,),
        in_specs=[pl.BlockSpec((tm, D), lambda i: (i, 3)),
                  pl.BlockSpec((tm, D), lambda i: (i, 4)),
                  pl.BlockSpec((1, 1, D), lambda i: (l, 0, 0)),
                  pl.BlockSpec((None, GMLP_GROUPS, GMLP_CHUNK, GMLP_CHUNK), lambda i: (l, 0, 0, 0)),
                  pl.BlockSpec((GMLP_GROUPS, GMLP_CHUNK, gw), lambda i: (0, 0, 0))],
        out_specs=pl.BlockSpec((tm, D), lambda i: (i, 0)),
        out_shape=jax.ShapeDtypeStruct((T, D), BF16),
        compiler_params=_params(1, 40),
        name="gmlp_spatial_gating",
    )(proj, proj, gmlp_g.reshape(-1, 1, D), w_s, bs)


def _ctx_attn_kernel(q_ref, k_ref, v_ref, o_ref, *, heads, dh):
    for h in range(heads):
        cols = slice(h * dh, (h + 1) * dh)
        s = lax.dot_general(q_ref[:, cols], k_ref[:, cols], _NT, preferred_element_type=F32)
        p = jnp.exp(s - jnp.max(s, axis=-1, keepdims=True))
        denom = jnp.sum(p, axis=-1, keepdims=True)
        o = jnp.dot(p.astype(BF16), v_ref[:, cols], preferred_element_type=F32)
        o_ref[:, cols] = (o / denom).astype(BF16)


def _context_attention(proj, batch, seq, D, heads, dh):
    return pl.pallas_call(
        functools.partial(_ctx_attn_kernel, heads=heads, dh=dh),
        grid=(batch,),
        in_specs=[pl.BlockSpec((seq, D), lambda b: (b, 0)),
                  pl.BlockSpec((seq, D), lambda b: (b, 1)),
                  pl.BlockSpec((seq, D), lambda b: (b, 2))],
        out_specs=pl.BlockSpec((seq, D), lambda b: (b, 0)),
        out_shape=jax.ShapeDtypeStruct((batch * seq, D), BF16),
        compiler_params=_params(1, 32),
        name="context_attention",
    )(proj, proj, proj)


def _na_kernel(q_ref, k_ref, v_ref, kc_ref, vc_ref, u_ref, mk_ref, o_ref, vt_scr, vct_scr, *, hps, dh, rows):
    m = pl.program_id(2)
    nblk = rows // NA_QROWS
    win = NA_KROWS * GRID_W
    nkb = win // NA_KBLOCK

    @pl.when(m == 0)
    def _():
        for kb in range(v_ref.shape[0] // NA_KBLOCK):
            vt_scr[kb] = v_ref[kb * NA_KBLOCK:(kb + 1) * NA_KBLOCK, :].astype(F32).T.astype(BF16)
        vct_scr[...] = vc_ref[0].astype(F32).T.astype(BF16)

    r0 = m * NA_QROWS
    ws = jnp.clip(r0 - WIN_ROWS // 2, 0, rows - NA_KROWS)
    krows = pl.ds(pl.multiple_of(ws * GRID_W, NA_KBLOCK), win)
    urows = pl.ds(pl.multiple_of((ws - r0 + WIN_ROWS) * GRID_W, NA_KBLOCK), win)
    kb0 = ws // (NA_KBLOCK // GRID_W)
    row_mask = mk_ref[jnp.where(m == 0, 0, jnp.where(m == nblk - 1, 2, 1))]
    for h in range(hps):
        cols = slice(h * dh, (h + 1) * dh)
        q = q_ref[:, cols]
        s_loc = (lax.dot_general(k_ref[krows, cols], q, _NT, preferred_element_type=F32)
                 + u_ref[0, h, urows, :] + row_mask)
        s_ctx = lax.dot_general(kc_ref[0, :, cols], q, _NT, preferred_element_type=F32)
        mx = jnp.maximum(jnp.max(s_loc, axis=0, keepdims=True), jnp.max(s_ctx, axis=0, keepdims=True))
        p_loc = jnp.exp(s_loc - mx)
        p_ctx = jnp.exp(s_ctx - mx)
        denom = jnp.sum(p_loc, axis=0, keepdims=True) + jnp.sum(p_ctx, axis=0, keepdims=True)
        p_loc = p_loc.astype(BF16)
        ot = jnp.dot(vct_scr[cols, :], p_ctx.astype(BF16), preferred_element_type=F32)
        for t in range(nkb):
            ot = ot + jnp.dot(vt_scr[kb0 + t, cols, :], p_loc[t * NA_KBLOCK:(t + 1) * NA_KBLOCK, :],
                              preferred_element_type=F32)
        o_ref[:, cols] = (ot / denom).T.astype(BF16)


def _na_tables(rpb, rows):
    L, H = rpb.shape[:2]
    kh = WIN_ROWS
    col = np.arange(GRID_W)
    cs = np.clip(col - WIN_COLS // 2, 0, GRID_W - WIN_COLS)
    col_valid = (col[:, None] >= cs[None, :]) & (col[:, None] < cs[None, :] + WIN_COLS)
    n_rho = NA_KROWS + WIN_ROWS
    n_dr = 2 * WIN_ROWS - 1
    j = np.arange(NA_QROWS)
    flipped = jnp.pad(jnp.flip(rpb.astype(F32), axis=-1), ((0, 0), (0, 0), (0, 0), (GRID_W, GRID_W)))
    t1 = jnp.stack([flipped[..., GRID_W + WIN_COLS - 1 - kc:2 * GRID_W + WIN_COLS - 1 - kc] for kc in range(GRID_W)],
                   axis=3)
    t1 = jnp.where(col_valid, t1, -jnp.inf)
    lanes = []
    for jj in range(NA_QROWS):
        dr = np.clip(np.arange(n_rho) - WIN_ROWS - jj + WIN_ROWS - 1, 0, n_dr - 1)
        lo, hi = int((dr == 0).sum()) - 1, int((dr == n_dr - 1).sum()) - 1
        assert lo + n_dr + hi == n_rho
        lanes.append(jnp.concatenate([t1[:, :, :1]] * lo + [t1] + [t1[:, :, -1:]] * hi, axis=2))
    u = jnp.concatenate(lanes, axis=-1).reshape(L, H, n_rho * GRID_W, NA_QROWS * GRID_W)

    def band(r0, ws):
        rs = np.clip(r0 + j - kh // 2, 0, rows - kh)
        key_row = ws + np.arange(NA_KROWS)
        ok = (key_row[:, None] >= rs[None, :]) & (key_row[:, None] < rs[None, :] + kh)
        mk = np.where(ok, 0.0, -np.inf).astype(np.float32)
        return np.broadcast_to(mk[:, None, :, None], (NA_KROWS, GRID_W, NA_QROWS, GRID_W)).reshape(
            NA_KROWS * GRID_W, NA_QROWS * GRID_W)

    row_mask = np.stack([band(0, 0), band(NA_QROWS, 0), band(rows - NA_QROWS, rows - NA_KROWS)])
    return u, jnp.asarray(row_mask)


def _neighbourhood_attention(proj, kc, vc, u, row_mask, l, nc, dec_batch, dec_seq, D, heads, dh):
    rows = dec_seq // GRID_W
    assert rows % NA_QROWS == 0 and rows >= NA_KROWS
    hps = min(NA_HEADS_PER_STEP, heads)
    qn = NA_QROWS * GRID_W
    cw = hps * dh
    n_hg = heads // hps
    nblk = rows // NA_QROWS
    q0 = nc // qn
    k0 = nc // dec_seq
    ncols = D // cw
    past = kc.shape[1]
    kernel = functools.partial(_na_kernel, hps=hps, dh=dh, rows=rows)
    return pl.pallas_call(
        kernel,
        grid=(dec_batch, n_hg, nblk),
        in_specs=[pl.BlockSpec((qn, cw), lambda b, g, m: (q0 + b * nblk + m, g)),
                  pl.BlockSpec((dec_seq, cw), lambda b, g, m: (k0 + b, ncols + g)),
                  pl.BlockSpec((dec_seq, cw), lambda b, g, m: (k0 + b, 2 * ncols + g)),
                  pl.BlockSpec((1, past, cw), lambda b, g, m: (b, 0, g)),
                  pl.BlockSpec((1, past, cw), lambda b, g, m: (b, 0, g)),
                  pl.BlockSpec((1, hps) + u.shape[2:], lambda b, g, m: (l, g, 0, 0)),
                  pl.BlockSpec(row_mask.shape, lambda b, g, m: (0, 0, 0))],
        out_specs=pl.BlockSpec((qn, cw), lambda b, g, m: (b * nblk + m, g)),
        out_shape=jax.ShapeDtypeStruct((dec_batch * dec_seq, D), BF16),
        scratch_shapes=[pltpu.VMEM((dec_seq // NA_KBLOCK, cw, NA_KBLOCK), BF16),
                        pltpu.VMEM((cw, past), BF16)],
        compiler_params=_params(3, 52),
        name="neighbourhood_attention",
    )(proj, proj, proj, kc, vc, u, row_mask)


def _merge_kernel(ya_ref, ybc_ref, ybl_ref, ga_ref, gb_ref, wa_ref, wb_ref, o_ref, *, ncb):
    yb = jnp.where(pl.program_id(0) < ncb, ybc_ref[...], ybl_ref[...])
    a = jnp.dot(ya_ref[...], wa_ref[0], preferred_element_type=F32)
    b = jnp.dot(yb, wb_ref[0], preferred_element_type=F32)
    o = _sigmoid(ga_ref[...].astype(F32)) * a + _sigmoid(gb_ref[...].astype(F32)) * b
    o_ref[...] = o.astype(BF16)


def _merge(ya, yb_ctx, yb_lat, proj, w_pa, w_pb, l, D, tm):
    T = ya.shape[0]
    tn = _tile(D, 512)
    nj = D // tn
    ncb = yb_ctx.shape[0] // tm
    return pl.pallas_call(
        functools.partial(_merge_kernel, ncb=ncb),
        grid=(T // tm, nj),
        in_specs=[pl.BlockSpec((tm, D), lambda i, j: (i, 0)),
                  pl.BlockSpec((tm, D), lambda i, j: (jnp.minimum(i, ncb - 1), 0)),
                  pl.BlockSpec((tm, D), lambda i, j: (jnp.maximum(i - ncb, 0), 0)),
                  pl.BlockSpec((tm, tn), lambda i, j: (i, 5 * nj + j)),
                  pl.BlockSpec((tm, tn), lambda i, j: (i, 6 * nj + j)),
                  pl.BlockSpec((1, D, tn), lambda i, j: (l, 0, j)),
                  pl.BlockSpec((1, D, tn), lambda i, j: (l, 0, j))],
        out_specs=pl.BlockSpec((tm, tn), lambda i, j: (i, j)),
        out_shape=jax.ShapeDtypeStruct((T, D), BF16),
        compiler_params=_params(2, 56),
        name="gated_branch_merge",
    )(ya, yb_ctx, yb_lat, proj, proj, w_pa, w_pb)


def _argmax_first(vals):
    best = vals[0]
    idx = jnp.zeros(best.shape, I32)
    for k in range(1, len(vals)):
        upd = vals[k] > best
        idx = jnp.where(upd, k, idx)
        best = jnp.where(upd, vals[k], best)
    return idx


def _select(idx, vals):
    out = vals[-1]
    for k in range(len(vals) - 2, -1, -1):
        out = jnp.where(idx == k, vals[k], out)
    return out


def _route(logits_t, router_b):
    m = jnp.max(logits_t, axis=0, keepdims=True)
    ex = jnp.exp(logits_t - m)
    probs = ex / jnp.sum(ex, axis=0, keepdims=True)
    sel = probs + router_b
    sel_rows = [sel[e:e + 1, :] for e in range(N_EXPERTS)]
    prob_rows = [probs[e:e + 1, :] for e in range(N_EXPERTS)]
    scores = []
    for g in range(N_EXPERT_GROUPS):
        a, b, c, d = sel_rows[g * EXPERTS_PER_GROUP:(g + 1) * EXPERTS_PER_GROUP]
        hi1, lo1 = jnp.maximum(a, b), jnp.minimum(a, b)
        hi2, lo2 = jnp.maximum(c, d), jnp.minimum(c, d)
        top1 = jnp.maximum(hi1, hi2)
        top2 = jnp.maximum(jnp.minimum(hi1, hi2), jnp.where(hi1 >= hi2, lo1, lo2))
        scores.append(top1 + top2)
    best_g = _argmax_first(scores)
    in_sel = [_select(best_g, [sel_rows[g * EXPERTS_PER_GROUP + k] for g in range(N_EXPERT_GROUPS)])
              for k in range(EXPERTS_PER_GROUP)]
    in_prob = [_select(best_g, [prob_rows[g * EXPERTS_PER_GROUP + k] for g in range(N_EXPERT_GROUPS)])
               for k in range(EXPERTS_PER_GROUP)]
    i1 = _argmax_first(in_sel)
    i2 = _argmax_first([jnp.where(i1 == k, -jnp.inf, in_sel[k]) for k in range(EXPERTS_PER_GROUP)])
    w1 = _select(i1, in_prob)
    w2 = _select(i2, in_prob)
    tot = w1 + w2
    base = best_g * EXPERTS_PER_GROUP
    return (base + i1, base + i2), (w1 / tot, w2 / tot)


def _out_kernel(mg_ref, x_ref, mod_ref, g_ref, w_ref, rw_ref, rb_ref, xo_ref, h_ref, e_ref, gt_ref, acc0, acc1):
    s = pl.program_id(0)

    @pl.when(s == 0)
    def _():
        acc1[...] = jnp.zeros(acc1.shape, acc1.dtype)

    def step(acc_w, acc_r):
        gate1 = mod_ref[0, 0, 2:3, :]
        shift2 = mod_ref[0, 0, 3:4, :]
        scale2 = 1.0 + mod_ref[0, 0, 4:5, :]
        g = g_ref[0]
        acc_w[...] = jnp.dot(mg_ref[...], w_ref[0], preferred_element_type=F32)
        for c in range(x_ref.shape[0] // ROW_CHUNK):
            rows = slice(c * ROW_CHUNK, (c + 1) * ROW_CHUNK)
            xn = x_ref[rows, :] + gate1 * acc_r[rows, :]
            xo_ref[rows, :] = xn
            h_ref[rows, :] = (_rms_scale(xn) * g * scale2 + shift2).astype(BF16)
        logits_t = lax.dot_general(rw_ref[...], h_ref[...], _NT, preferred_element_type=F32)
        experts, gates = _route(logits_t, rb_ref[...])
        for k in range(TOP_K):
            e_ref[k:k + 1, :] = experts[k]
            gt_ref[k:k + 1, :] = gates[k]

    pl.when(s % 2 == 0)(lambda: step(acc0, acc1))
    pl.when(s % 2 == 1)(lambda: step(acc1, acc0))


def _output_projection(merged, x, mod, norm_g, w_out, router_wt, router_b, l, cond_of_block, tm):
    T, D = x.shape
    E = router_wt.shape[0]
    n = T // tm
    cur = lambda s: jnp.minimum(s, n - 1)
    prev = lambda s: jnp.maximum(s - 1, 0)
    return pl.pallas_call(
        _out_kernel,
        grid=(n + 1,),
        in_specs=[pl.BlockSpec((tm, D), lambda s: (cur(s), 0)),
                  pl.BlockSpec((tm, D), lambda s: (prev(s), 0)),
                  pl.BlockSpec((1, 1, N_MOD, D), lambda s: (l, cond_of_block(prev(s)), 0, 0)),
                  pl.BlockSpec((1, 1, D), lambda s: (l, 0, 0)),
                  pl.BlockSpec((1, D, D), lambda s: (l, 0, 0)),
                  pl.BlockSpec((E, D), lambda s: (0, 0)),
                  pl.BlockSpec((E, 1), lambda s: (0, 0))],
        out_specs=[pl.BlockSpec((tm, D), lambda s: (prev(s), 0)),
                   pl.BlockSpec((tm, D), lambda s: (prev(s), 0)),
                   pl.BlockSpec((TOP_K, tm), lambda s: (0, prev(s))),
                   pl.BlockSpec((TOP_K, tm), lambda s: (0, prev(s)))],
        out_shape=[jax.ShapeDtypeStruct((T, D), F32),
                   jax.ShapeDtypeStruct((T, D), BF16),
                   jax.ShapeDtypeStruct((TOP_K, T), I32),
                   jax.ShapeDtypeStruct((TOP_K, T), F32)],
        scratch_shapes=[pltpu.VMEM((tm, D), F32), pltpu.VMEM((tm, D), F32)],
        compiler_params=_params(1, 56),
        name="output_projection_norm2_router",
    )(merged, x, mod, norm_g.reshape(-1, 1, D), w_out, router_wt, router_b.reshape(E, 1))


def _moe_kernel(blk_ref, exp_ref, lo_ref, hi_ref, x_ref, wg_ref, wu_ref, wd_ref, o_ref):
    del exp_ref
    w = pl.program_id(0)
    lo, hi = lo_ref[w], hi_ref[w]
    row0 = blk_ref[w] * MOE_BLOCK

    @pl.when(lo == row0)
    def _():
        o_ref[...] = jnp.zeros(o_ref.shape, o_ref.dtype)

    @pl.when(hi > lo)
    def _():
        x = x_ref[...]
        g = jnp.dot(x, wg_ref[0], preferred_element_type=F32)
        u = jnp.dot(x, wu_ref[0], preferred_element_type=F32)
        hid = (g * _sigmoid(g)) * u
        y = jnp.dot(hid.astype(BF16), wd_ref[0], preferred_element_type=F32)
        row = row0 + lax.broadcasted_iota(I32, (MOE_BLOCK, 1), 0)
        o_ref[...] += jnp.where((row >= lo) & (row < hi), y, 0.0).astype(o_ref.dtype)


def _moe_experts(xs, items, w_gate, w_up, w_down, l):
    A, D = xs.shape
    E, DE = w_gate.shape[1], w_gate.shape[3]
    w_gate = w_gate.reshape(-1, D, DE)
    w_up = w_up.reshape(-1, D, DE)
    w_down = w_down.reshape(-1, DE, D)
    n_items = items[0].shape[0]
    grid_spec = pltpu.PrefetchScalarGridSpec(
        num_scalar_prefetch=4,
        grid=(n_items,),
        in_specs=[pl.BlockSpec((MOE_BLOCK, D), lambda w, blk, ex, lo, hi: (blk[w], 0)),
                  pl.BlockSpec((1, D, DE), lambda w, blk, ex, lo, hi: (l * E + ex[w], 0, 0)),
                  pl.BlockSpec((1, D, DE), lambda w, blk, ex, lo, hi: (l * E + ex[w], 0, 0)),
                  pl.BlockSpec((1, DE, D), lambda w, blk, ex, lo, hi: (l * E + ex[w], 0, 0))],
        out_specs=pl.BlockSpec((MOE_BLOCK, D), lambda w, blk, ex, lo, hi: (blk[w], 0)),
    )
    return pl.pallas_call(
        _moe_kernel,
        grid_spec=grid_spec,
        out_shape=jax.ShapeDtypeStruct((A, D), BF16),
        compiler_params=_params(1, 48),
        name="moe_experts",
    )(*items, xs, w_gate, w_up, w_down)


def _dispatch_plan(experts, T):
    A = TOP_K * T
    nb = A // MOE_BLOCK
    n_items = nb + N_EXPERTS - 1
    e_flat = experts.reshape(A)
    iota = jnp.arange(A, dtype=I32)
    _, order = lax.sort((e_flat, iota), num_keys=1, is_stable=True)
    _, rank = lax.sort((order, iota), num_keys=1)
    eid = jnp.arange(N_EXPERTS, dtype=I32)
    counts = jnp.sum((e_flat[:, None] == eid[None, :]).astype(I32), axis=0)
    ends = jnp.cumsum(counts)
    starts = ends - counts
    first_blk = starts // MOE_BLOCK
    n_blk = jnp.where(counts > 0, (ends - 1) // MOE_BLOCK - first_blk + 1, 0)
    item_end = jnp.cumsum(n_blk)
    item_start = item_end - n_blk
    w = jnp.arange(n_items, dtype=I32)
    valid = w < item_end[-1]
    ex = jnp.minimum(jnp.sum((w[:, None] >= item_end[None, :]).astype(I32), axis=1), N_EXPERTS - 1)
    pick = lambda tbl: jnp.sum(jnp.where(ex[:, None] == eid[None, :], tbl[None, :], 0), axis=1)
    blk = jnp.where(valid, pick(first_blk) + w - pick(item_start), nb - 1)
    lo = jnp.where(valid, jnp.maximum(pick(starts), blk * MOE_BLOCK), 0)
    hi = jnp.where(valid, jnp.minimum(pick(ends), (blk + 1) * MOE_BLOCK), 0)
    ex = jnp.where(valid, ex, jnp.max(jnp.where(counts > 0, eid, 0)))
    return order % T, rank, (blk.astype(I32), ex.astype(I32), lo.astype(I32), hi.astype(I32))


def _combined(x_ref, y1_ref, y2_ref, g1_ref, g2_ref, gate2, rows):
    y = g1_ref[rows, :] * y1_ref[rows, :].astype(F32) + g2_ref[rows, :] * y2_ref[rows, :].astype(F32)
    return x_ref[rows, :] + gate2 * y


def _combine_kernel(x_ref, y1_ref, y2_ref, g1_ref, g2_ref, mod_ref, modn_ref, gn_ref, xo_ref, h_ref):
    gate2 = mod_ref[0, 0, 5:6, :]
    scale1 = 1.0 + modn_ref[0, 0, 1:2, :]
    shift1 = modn_ref[0, 0, 0:1, :]
    g = gn_ref[0]

    def chunk(rows):
        xn = _combined(x_ref, y1_ref, y2_ref, g1_ref, g2_ref, gate2, rows)
        xo_ref[rows, :] = xn
        h_ref[rows, :] = (_rms_scale(xn) * g * scale1 + shift1).astype(BF16)

    _for_chunks(x_ref.shape[0], ROW_CHUNK, chunk)


def _combine_final_kernel(x_ref, y1_ref, y2_ref, g1_ref, g2_ref, mod_ref, gf_ref, yp_ref, ys_ref, *, ncb):
    i = pl.program_id(0)
    gate2 = mod_ref[0, 0, 5:6, :]
    g = gf_ref[...]

    def run(dst_ref):
        def chunk(rows):
            xn = _combined(x_ref, y1_ref, y2_ref, g1_ref, g2_ref, gate2, rows)
            dst_ref[rows, :] = _rms_scale(xn) * g

        _for_chunks(x_ref.shape[0], ROW_CHUNK, chunk)

    pl.when(i < ncb)(lambda: run(yp_ref))
    pl.when(i >= ncb)(lambda: run(ys_ref))


def _combine(x, y1, y2, g1, g2, mod, norm1_g, l, cond_of_block, tm):
    T, D = x.shape
    row = pl.BlockSpec((tm, D), lambda i: (i, 0))
    col = pl.BlockSpec((tm, 1), lambda i: (i, 0))
    return pl.pallas_call(
        _combine_kernel,
        grid=(T // tm,),
        in_specs=[row, row, row, col, col,
                  pl.BlockSpec((1, 1, N_MOD, D), lambda i: (l, cond_of_block(i), 0, 0)),
                  pl.BlockSpec((1, 1, N_MOD, D), lambda i: (l + 1, cond_of_block(i), 0, 0)),
                  pl.BlockSpec((1, 1, D), lambda i: (l + 1, 0, 0))],
        out_specs=[row, row],
        out_shape=[jax.ShapeDtypeStruct((T, D), F32), jax.ShapeDtypeStruct((T, D), BF16)],
        compiler_params=_params(1, 52),
        name="moe_combine_residual_norm1",
    )(x, y1, y2, g1, g2, mod, mod, norm1_g.reshape(-1, 1, D))


def _combine_final(x, y1, y2, g1, g2, mod, final_g, l, nc, cond_of_block, tm):
    T, D = x.shape
    ncb = nc // tm
    row = pl.BlockSpec((tm, D), lambda i: (i, 0))
    col = pl.BlockSpec((tm, 1), lambda i: (i, 0))
    return pl.pallas_call(
        functools.partial(_combine_final_kernel, ncb=ncb),
        grid=(T // tm,),
        in_specs=[row, row, row, col, col,
                  pl.BlockSpec((1, 1, N_MOD, D), lambda i: (l, cond_of_block(i), 0, 0)),
                  pl.BlockSpec((1, D), lambda i: (0, 0))],
        out_specs=[pl.BlockSpec((tm, D), lambda i: (jnp.minimum(i, ncb - 1), 0)),
                   pl.BlockSpec((tm, D), lambda i: (jnp.maximum(i - ncb, 0), 0))],
        out_shape=[jax.ShapeDtypeStruct((nc, D), F32), jax.ShapeDtypeStruct((T - nc, D), F32)],
        compiler_params=_params(1, 52),
        name="moe_combine_residual_final_norm",
    )(x, y1, y2, g1, g2, mod, final_g.reshape(1, D))


def kernel(x_prompt, x_sample, cache_k, cache_v, c, c_ctx, w_ada, b_ada, norm1_g, w_in, gmlp_g, w_s, b_s, rpb,
           w_proj_a, w_proj_b, w_out, norm2_g, router_w, router_b, w_gate, w_up, w_down, final_g):
    batch, seq, D = x_prompt.shape
    dec_batch, dec_seq, _ = x_sample.shape
    depth = w_in.shape[0]
    heads, dh = cache_k.shape[3], cache_k.shape[4]
    past = cache_k.shape[2]
    nc, nl = batch * seq, dec_batch * dec_seq
    T = nc + nl
    assert heads * dh == D and gmlp_g.shape[1] == D and w_in.shape[2] == 7 * D
    assert 1 + dec_batch <= COND_ROWS
    assert router_w.shape[1] == N_EXPERTS
    assert seq % GMLP_CHUNK == 0 and dec_seq % GMLP_CHUNK == 0 and dec_seq % GRID_W == 0
    assert nc % dec_seq == 0 and (TOP_K * T) % MOE_BLOCK == 0

    tm_big = 1024 if (nc % 1024 == 0 and dec_seq % 1024 == 0) else GMLP_CHUNK
    tm_mid = 512 if (nc % 512 == 0 and dec_seq % 512 == 0) else GMLP_CHUNK

    def cond_of(tm):
        ncb, per = nc // tm, dec_seq // tm
        return lambda i: jnp.where(i < ncb, 0, 1 + (i - ncb) // per)

    cond = jnp.zeros((COND_ROWS, D), F32).at[0].set(c_ctx).at[1:1 + dec_batch].set(c)
    mod = _modulation_all(cond, w_ada, b_ada)

    w_s_b = w_s.astype(BF16)
    w_pa_b = w_proj_a.astype(BF16)
    w_pb_b = w_proj_b.astype(BF16)
    w_out_b = w_out.astype(BF16)
    w_gate_b = w_gate.astype(BF16)
    w_up_b = w_up.astype(BF16)
    w_down_b = w_down.astype(BF16)
    kc_all = cache_k.astype(BF16).reshape(dec_batch, depth, past, D)
    vc_all = cache_v.astype(BF16).reshape(dec_batch, depth, past, D)
    router_wt = router_w.T.astype(BF16)
    u_all, row_mask = _na_tables(rpb, dec_seq // GRID_W)

    x, h = _prologue(x_prompt.reshape(nc, D), x_sample.reshape(nl, D), mod, norm1_g, cond_of(tm_mid), tm_mid)
    new_k = jnp.zeros((batch, depth, seq, D), F32)
    new_v = jnp.zeros((batch, depth, seq, D), F32)
    for l in range(depth):
        proj, new_k, new_v = _input_projection(h, w_in, new_k, new_v, l, tm_big, dh ** -0.5)
        ya = _gmlp(proj, gmlp_g, w_s_b, b_s, l, D, tm_mid)
        yb_ctx = _context_attention(proj, batch, seq, D, heads, dh)
        yb_lat = _neighbourhood_attention(proj, kc_all[:, l], vc_all[:, l], u_all, row_mask, l, nc, dec_batch,
                                          dec_seq, D, heads, dh)
        merged = _merge(ya, yb_ctx, yb_lat, proj, w_pa_b, w_pb_b, l, D, tm_big)
        x, h2, experts, gates = _output_projection(merged, x, mod, norm2_g, w_out_b, router_wt, router_b, l,
                                                   cond_of(tm_mid), tm_mid)
        tok_sorted, rank, items = _dispatch_plan(experts, T)
        yexp = _moe_experts(h2[tok_sorted], items, w_gate_b, w_up_b, w_down_b, l)
        y1, y2, g1, g2 = yexp[rank[:T]], yexp[rank[T:]], gates[0][:, None], gates[1][:, None]
        if l + 1 < depth:
            x, h = _combine(x, y1, y2, g1, g2, mod, norm1_g, l, cond_of(tm_mid), tm_mid)
        else:
            y_prompt, y_sample = _combine_final(x, y1, y2, g1, g2, mod, final_g, l, nc, cond_of(tm_mid), tm_mid)

    return (y_prompt.reshape(batch, seq, D), y_sample.reshape(dec_batch, dec_seq, D),
            new_k.reshape(batch, depth, seq, heads, dh), new_v.reshape(batch, depth, seq, heads, dh))
```

```python
import functools

import numpy as np
import jax
import jax.numpy as jnp
from jax import lax
from jax.experimental import pallas as pl
from jax.experimental.pallas import tpu as pltpu

F32 = jnp.float32
BF16 = jnp.bfloat16
I32 = jnp.int32

EPS = 1e-6
N_MOD = 6
GRID_W = 64
WIN_ROWS = 8
WIN_COLS = 16
GMLP_CHUNK = 128
GMLP_GROUPS = 8
N_EXPERTS = 16
N_EXPERT_GROUPS = 4
EXPERTS_PER_GROUP = N_EXPERTS // N_EXPERT_GROUPS
TOP_K = 2
MOE_BLOCK = 256
COND_ROWS = 8
ROW_CHUNK = 64
CAST_CHUNK = 256
NA_QROWS = 4
NA_KROWS = 12
NA_KBLOCK = 256
NA_ONES_ROWS = 16
NA_HEADS_PER_STEP = 4
LOG2E = 1.4426950408889634
MIB = 1024 * 1024

_NT = (((1,), (1,)), ((), ()))


def _tile(n, pref):
    return pref if n % pref == 0 else n


def _params(n_axes, vmem_mib):
    return pltpu.CompilerParams(dimension_semantics=("arbitrary",) * n_axes,
                                vmem_limit_bytes=vmem_mib * MIB)


def _for_chunks(n, c, fn):
    c = min(c, n)

    def body(t, carry):
        fn(pl.ds(pl.multiple_of(t * c, c), c))
        return carry

    lax.fori_loop(0, n // c, body, 0)


def _sigmoid(x):
    return 1.0 / (1.0 + jnp.exp(-x))


def _gelu_tanh(x):
    return x * (0.5 * (1.0 + jnp.tanh(0.7978845608028654 * (x + 0.044715 * (x * x * x)))))


def _rms_scale(x):
    return x * lax.rsqrt(jnp.mean(x * x, axis=-1, keepdims=True) + EPS)


def _ada_kernel(c_ref, w_ref, b_ref, o_ref):
    c = c_ref[...]
    s = (c * _sigmoid(c)).astype(BF16)
    o_ref[0] = jnp.dot(s, w_ref[0].astype(BF16), preferred_element_type=F32) + b_ref[0]


def _modulation_all(cond, w_ada, b_ada):
    L, D, N = w_ada.shape
    tn = _tile(N, 1024)
    out = pl.pallas_call(
        _ada_kernel,
        grid=(L, N // tn),
        in_specs=[pl.BlockSpec((COND_ROWS, D), lambda l, j: (0, 0)),
                  pl.BlockSpec((1, D, tn), lambda l, j: (l, 0, j)),
                  pl.BlockSpec((1, 1, tn), lambda l, j: (l, 0, j))],
        out_specs=pl.BlockSpec((1, COND_ROWS, tn), lambda l, j: (l, 0, j)),
        out_shape=jax.ShapeDtypeStruct((L, COND_ROWS, N), F32),
        compiler_params=_params(2, 40),
        name="ada_modulation",
    )(cond, w_ada, b_ada.reshape(L, 1, N))
    return out.reshape(L, COND_ROWS, N_MOD, D)


def _prologue_kernel(xp_ref, xs_ref, mod_ref, g_ref, x_ref, h_ref, *, ncb):
    i = pl.program_id(0)
    g = g_ref[0]
    scale1 = 1.0 + mod_ref[0, 0, 1:2, :]
    shift1 = mod_ref[0, 0, 0:1, :]

    def run(src_ref):
        def chunk(rows):
            x = src_ref[rows, :]
            x_ref[rows, :] = x
            h_ref[rows, :] = (_rms_scale(x) * g * scale1 + shift1).astype(BF16)

        _for_chunks(x_ref.shape[0], ROW_CHUNK, chunk)

    pl.when(i < ncb)(lambda: run(xp_ref))
    pl.when(i >= ncb)(lambda: run(xs_ref))


def _prologue(xp, xs, mod, norm_g, cond_of_block, tm):
    nc, D = xp.shape
    T = nc + xs.shape[0]
    ncb = nc // tm
    row = pl.BlockSpec((tm, D), lambda i: (i, 0))
    return pl.pallas_call(
        functools.partial(_prologue_kernel, ncb=ncb),
        grid=(T // tm,),
        in_specs=[pl.BlockSpec((tm, D), lambda i: (jnp.minimum(i, ncb - 1), 0)),
                  pl.BlockSpec((tm, D), lambda i: (jnp.maximum(i - ncb, 0), 0)),
                  pl.BlockSpec((1, 1, N_MOD, D), lambda i: (0, cond_of_block(i), 0, 0)),
                  pl.BlockSpec((1, 1, D), lambda i: (0, 0, 0))],
        out_specs=[row, row],
        out_shape=[jax.ShapeDtypeStruct((T, D), F32), jax.ShapeDtypeStruct((T, D), BF16)],
        compiler_params=_params(1, 40),
        name="concat_norm1",
    )(xp, xs, mod, norm_g.reshape(-1, 1, D))


def _proj_kernel(h_ref, w_ref, nk_in, nv_in, o_ref, k_ref, v_ref, w_scr, *, ncb, jk0, nk, qscale):
    del nk_in, nv_in
    j = pl.program_id(0)
    i = pl.program_id(1)

    @pl.when(i == 0)
    def _():
        def chunk(rows):
            w_scr[rows, :] = w_ref[0, rows, :].astype(BF16)

        _for_chunks(w_scr.shape[0], CAST_CHUNK, chunk)

    acc = jnp.dot(h_ref[...], w_scr[...], preferred_element_type=F32)
    o_ref[...] = (acc * jnp.where(j < jk0, qscale, 1.0)).astype(BF16)

    def put(dst_ref):
        seq = dst_ref.shape[2]
        for b in range(dst_ref.shape[0]):
            dst_ref[b, 0] = acc[b * seq:(b + 1) * seq, :]

    pl.when((i < ncb) & (j >= jk0) & (j < jk0 + nk))(lambda: put(k_ref))
    pl.when((i < ncb) & (j >= jk0 + nk) & (j < jk0 + 2 * nk))(lambda: put(v_ref))


def _input_projection(h, w_in, new_k, new_v, l, tm, qscale):
    T, D = h.shape
    N = w_in.shape[-1]
    batch, _, seq, _ = new_k.shape
    tn = _tile(D, 1024)
    ncb = batch * seq // tm
    jk0, nk = D // tn, D // tn
    kernel = functools.partial(_proj_kernel, ncb=ncb, jk0=jk0, nk=nk, qscale=qscale)

    def cache_block(j0):
        def index(j, i):
            inside = (j >= j0) & (j < j0 + nk)
            row = jnp.where(j < j0, 0, jnp.where(inside, jnp.minimum(i, ncb - 1), ncb - 1))
            col = jnp.where(j < j0, 0, jnp.where(inside, j - j0, nk - 1))
            return (row, l, 0, col)
        return pl.BlockSpec((tm // seq, 1, seq, tn), index)

    return pl.pallas_call(
        kernel,
        grid=(N // tn, T // tm),
        in_specs=[pl.BlockSpec((tm, D), lambda j, i: (i, 0)),
                  pl.BlockSpec((1, D, tn), lambda j, i: (l, 0, j)),
                  pl.BlockSpec(memory_space=pl.ANY),
                  pl.BlockSpec(memory_space=pl.ANY)],
        out_specs=[pl.BlockSpec((tm, tn), lambda j, i: (i, j)), cache_block(jk0), cache_block(jk0 + nk)],
        out_shape=[jax.ShapeDtypeStruct((T, N), BF16),
                   jax.ShapeDtypeStruct(new_k.shape, F32),
                   jax.ShapeDtypeStruct(new_v.shape, F32)],
        scratch_shapes=[pltpu.VMEM((D, tn), BF16)],
        input_output_aliases={2: 1, 3: 2},
        compiler_params=_params(2, 60),
        name="input_projection",
    )(h, w_in, new_k, new_v)


def _gmlp_kernel(u_ref, gv_ref, g_ref, ws_ref, bs_ref, o_ref):
    gw = u_ref.shape[1] // GMLP_GROUPS
    g = g_ref[0]

    def chunk(rows):
        v = _rms_scale(_gelu_tanh(gv_ref[rows, :].astype(F32))) * g
        v = v.astype(BF16)
        for grp in range(GMLP_GROUPS):
            cols = slice(grp * gw, (grp + 1) * gw)
            s = jnp.dot(ws_ref[grp], v[:, cols], preferred_element_type=F32) + bs_ref[grp]
            u = _gelu_tanh(u_ref[rows, cols].astype(F32))
            o_ref[rows, cols] = (u * s).astype(BF16)

    _for_chunks(u_ref.shape[0], GMLP_CHUNK, chunk)


def _gmlp(proj, gmlp_g, w_s, b_s, l, D, tm):
    T = proj.shape[0]
    gw = D // GMLP_GROUPS
    bs = jnp.broadcast_to(b_s[l][:, :, None], (GMLP_GROUPS, GMLP_CHUNK, gw))
    return pl.pallas_call(
        _gmlp_kernel,
        grid=(T // tm,),
        in_specs=[pl.BlockSpec((tm, D), lambda i: (i, 3)),
                  pl.BlockSpec((tm, D), lambda i: (i, 4)),
                  pl.BlockSpec((1, 1, D), lambda i: (l, 0, 0)),
                  pl.BlockSpec((None, GMLP_GROUPS, GMLP_CHUNK, GMLP_CHUNK), lambda i: (l, 0, 0, 0)),
                  pl.BlockSpec((GMLP_GROUPS, GMLP_CHUNK, gw), lambda i: (0, 0, 0))],
        out_specs=pl.BlockSpec((tm, D), lambda i: (i, 0)),
        out_shape=jax.ShapeDtypeStruct((T, D), BF16),
        compiler_params=_params(1, 40),
        name="gmlp_spatial_gating",
    )(proj, proj, gmlp_g.reshape(-1, 1, D), w_s, bs)


def _ctx_attn_kernel(q_ref, k_ref, v_ref, o_ref, *, heads, dh):
    for h in range(heads):
        cols = slice(h * dh, (h + 1) * dh)
        s = lax.dot_general(q_ref[:, cols], k_ref[:, cols], _NT, preferred_element_type=F32)
        p = jnp.exp2(s - jnp.max(s, axis=-1, keepdims=True))
        denom = jnp.sum(p, axis=-1, keepdims=True)
        o = jnp.dot(p.astype(BF16), v_ref[:, cols], preferred_element_type=F32)
        o_ref[:, cols] = (o / denom).astype(BF16)


def _context_attention(proj, batch, seq, D, heads, dh):
    return pl.pallas_call(
        functools.partial(_ctx_attn_kernel, heads=heads, dh=dh),
        grid=(batch,),
        in_specs=[pl.BlockSpec((seq, D), lambda b: (b, 0)),
                  pl.BlockSpec((seq, D), lambda b: (b, 1)),
                  pl.BlockSpec((seq, D), lambda b: (b, 2))],
        out_specs=pl.BlockSpec((seq, D), lambda b: (b, 0)),
        out_shape=jax.ShapeDtypeStruct((batch * seq, D), BF16),
        compiler_params=_params(1, 32),
        name="context_attention",
    )(proj, proj, proj)


def _na_kernel(q_ref, k_ref, v_ref, kc_ref, vc_ref, um_ref, o_ref, vt_scr, vct_scr, *, hps, dh, rows):
    m = pl.program_id(2)
    win = NA_KROWS * GRID_W
    nkb = win // NA_KBLOCK
    hr = dh + NA_ONES_ROWS

    @pl.when(m == 0)
    def _():
        def fill(dst, vt):
            for h in range(hps):
                dst[h * hr:h * hr + dh, :] = vt[h * dh:(h + 1) * dh, :]
                dst[h * hr + dh:(h + 1) * hr, :] = jnp.ones((NA_ONES_ROWS, vt.shape[1]), BF16)

        for kb in range(v_ref.shape[0] // NA_KBLOCK):
            fill(vt_scr.at[kb], v_ref[kb * NA_KBLOCK:(kb + 1) * NA_KBLOCK, :].astype(F32).T.astype(BF16))
        fill(vct_scr, vc_ref[0].astype(F32).T.astype(BF16))

    ws = jnp.clip(m * NA_QROWS - WIN_ROWS // 2, 0, rows - NA_KROWS)
    krows = pl.ds(pl.multiple_of(ws * GRID_W, NA_KBLOCK), win)
    kb0 = ws // (NA_KBLOCK // GRID_W)
    scores = []
    for h in range(hps):
        cols = slice(h * dh, (h + 1) * dh)
        q = q_ref[:, cols]
        s_loc = lax.dot_general(k_ref[krows, cols], q, _NT, preferred_element_type=F32) + um_ref[0, 0, h]
        s_ctx = lax.dot_general(kc_ref[0, :, cols], q, _NT, preferred_element_type=F32)
        scores.append((s_loc, s_ctx))
    probs = []
    for s_loc, s_ctx in scores:
        mx = jnp.maximum(jnp.max(s_loc, axis=0, keepdims=True), jnp.max(s_ctx, axis=0, keepdims=True))
        probs.append((jnp.exp2(s_loc - mx).astype(BF16), jnp.exp2(s_ctx - mx).astype(BF16)))
    for h, (p_loc, p_ctx) in enumerate(probs):
        hrows = slice(h * hr, (h + 1) * hr)
        ot = jnp.dot(vct_scr[hrows, :], p_ctx, preferred_element_type=F32)
        for t in range(nkb):
            ot = ot + jnp.dot(vt_scr[kb0 + t, hrows, :], p_loc[t * NA_KBLOCK:(t + 1) * NA_KBLOCK, :],
                              preferred_element_type=F32)
        o_ref[:, h * dh:(h + 1) * dh] = (ot[:dh] / ot[dh:dh + 1]).T.astype(BF16)


def _na_tables(rpb, rows):
    L, H = rpb.shape[:2]
    kh = WIN_ROWS
    col = np.arange(GRID_W)
    cs = np.clip(col - WIN_COLS // 2, 0, GRID_W - WIN_COLS)
    col_valid = (col[:, None] >= cs[None, :]) & (col[:, None] < cs[None, :] + WIN_COLS)
    n_rho = NA_KROWS + WIN_ROWS
    n_dr = 2 * WIN_ROWS - 1
    j = np.arange(NA_QROWS)
    flipped = jnp.pad(jnp.flip(rpb.astype(F32), axis=-1), ((0, 0), (0, 0), (0, 0), (GRID_W, GRID_W)))
    t1 = jnp.stack([flipped[..., GRID_W + WIN_COLS - 1 - kc:2 * GRID_W + WIN_COLS - 1 - kc] for kc in range(GRID_W)],
                   axis=3)
    t1 = jnp.where(col_valid, t1, -jnp.inf)
    lanes = []
    for jj in range(NA_QROWS):
        dr = np.clip(np.arange(n_rho) - WIN_ROWS - jj + WIN_ROWS - 1, 0, n_dr - 1)
        lo, hi = int((dr == 0).sum()) - 1, int((dr == n_dr - 1).sum()) - 1
        assert lo + n_dr + hi == n_rho
        lanes.append(jnp.concatenate([t1[:, :, :1]] * lo + [t1] + [t1[:, :, -1:]] * hi, axis=2))
    u = jnp.concatenate(lanes, axis=-1).reshape(L, H, n_rho * GRID_W, NA_QROWS * GRID_W)

    def band(r0, ws):
        rs = np.clip(r0 + j - kh // 2, 0, rows - kh)
        key_row = ws + np.arange(NA_KROWS)
        ok = (key_row[:, None] >= rs[None, :]) & (key_row[:, None] < rs[None, :] + kh)
        mk = np.where(ok, 0.0, -np.inf).astype(np.float32)
        return np.broadcast_to(mk[:, None, :, None], (NA_KROWS, GRID_W, NA_QROWS, GRID_W)).reshape(
            NA_KROWS * GRID_W, NA_QROWS * GRID_W)

    win = NA_KROWS * GRID_W
    kinds = []
    for r0, ws in ((0, 0), (NA_QROWS, 0), (rows - NA_QROWS, rows - NA_KROWS)):
        start = (ws - r0 + WIN_ROWS) * GRID_W
        kinds.append(u[:, :, start:start + win] * LOG2E + jnp.asarray(band(r0, ws)))
    return jnp.stack(kinds, axis=1)


def _neighbourhood_attention(proj, kc, vc, um, l, nc, dec_batch, dec_seq, D, heads, dh):
    rows = dec_seq // GRID_W
    assert rows % NA_QROWS == 0 and rows >= NA_KROWS
    hps = min(NA_HEADS_PER_STEP, heads)
    qn = NA_QROWS * GRID_W
    cw = hps * dh
    n_hg = heads // hps
    nblk = rows // NA_QROWS
    q0 = nc // qn
    k0 = nc // dec_seq
    ncols = D // cw
    past = kc.shape[1]
    hr = dh + NA_ONES_ROWS
    kind = lambda m: jnp.where(m == 0, 0, jnp.where(m == nblk - 1, 2, 1))
    kernel = functools.partial(_na_kernel, hps=hps, dh=dh, rows=rows)
    return pl.pallas_call(
        kernel,
        grid=(dec_batch, n_hg, nblk),
        in_specs=[pl.BlockSpec((qn, cw), lambda b, g, m: (q0 + b * nblk + m, g)),
                  pl.BlockSpec((dec_seq, cw), lambda b, g, m: (k0 + b, ncols + g)),
                  pl.BlockSpec((dec_seq, cw), lambda b, g, m: (k0 + b, 2 * ncols + g)),
                  pl.BlockSpec((1, past, cw), lambda b, g, m: (b, 0, g)),
                  pl.BlockSpec((1, past, cw), lambda b, g, m: (b, 0, g)),
                  pl.BlockSpec((1, 1, hps) + um.shape[3:], lambda b, g, m: (l, kind(m), g, 0, 0))],
        out_specs=pl.BlockSpec((qn, cw), lambda b, g, m: (b * nblk + m, g)),
        out_shape=jax.ShapeDtypeStruct((dec_batch * dec_seq, D), BF16),
        scratch_shapes=[pltpu.VMEM((dec_seq // NA_KBLOCK, hps * hr, NA_KBLOCK), BF16),
                        pltpu.VMEM((hps * hr, past), BF16)],
        compiler_params=_params(3, 52),
        name="neighbourhood_attention",
    )(proj, proj, proj, kc, vc, um)


def _merge_kernel(ya_ref, ybc_ref, ybl_ref, ga_ref, gb_ref, wa_ref, wb_ref, o_ref, *, ncb):
    yb = jnp.where(pl.program_id(0) < ncb, ybc_ref[...], ybl_ref[...])
    a = jnp.dot(ya_ref[...], wa_ref[0], preferred_element_type=F32)
    b = jnp.dot(yb, wb_ref[0], preferred_element_type=F32)
    o = _sigmoid(ga_ref[...].astype(F32)) * a + _sigmoid(gb_ref[...].astype(F32)) * b
    o_ref[...] = o.astype(BF16)


def _merge(ya, yb_ctx, yb_lat, proj, w_pa, w_pb, l, D, tm):
    T = ya.shape[0]
    tn = _tile(D, 512)
    nj = D // tn
    ncb = yb_ctx.shape[0] // tm
    return pl.pallas_call(
        functools.partial(_merge_kernel, ncb=ncb),
        grid=(T // tm, nj),
        in_specs=[pl.BlockSpec((tm, D), lambda i, j: (i, 0)),
                  pl.BlockSpec((tm, D), lambda i, j: (jnp.minimum(i, ncb - 1), 0)),
                  pl.BlockSpec((tm, D), lambda i, j: (jnp.maximum(i - ncb, 0), 0)),
                  pl.BlockSpec((tm, tn), lambda i, j: (i, 5 * nj + j)),
                  pl.BlockSpec((tm, tn), lambda i, j: (i, 6 * nj + j)),
                  pl.BlockSpec((1, D, tn), lambda i, j: (l, 0, j)),
                  pl.BlockSpec((1, D, tn), lambda i, j: (l, 0, j))],
        out_specs=pl.BlockSpec((tm, tn), lambda i, j: (i, j)),
        out_shape=jax.ShapeDtypeStruct((T, D), BF16),
        compiler_params=_params(2, 56),
        name="gated_branch_merge",
    )(ya, yb_ctx, yb_lat, proj, proj, w_pa, w_pb)


def _argmax_first(vals):
    best = vals[0]
    idx = jnp.zeros(best.shape, I32)
    for k in range(1, len(vals)):
        upd = vals[k] > best
        idx = jnp.where(upd, k, idx)
        best = jnp.where(upd, vals[k], best)
    return idx


def _select(idx, vals):
    out = vals[-1]
    for k in range(len(vals) - 2, -1, -1):
        out = jnp.where(idx == k, vals[k], out)
    return out


def _route(logits_t, router_b):
    m = jnp.max(logits_t, axis=0, keepdims=True)
    ex = jnp.exp(logits_t - m)
    probs = ex / jnp.sum(ex, axis=0, keepdims=True)
    sel = probs + router_b
    sel_rows = [sel[e:e + 1, :] for e in range(N_EXPERTS)]
    prob_rows = [probs[e:e + 1, :] for e in range(N_EXPERTS)]
    scores = []
    for g in range(N_EXPERT_GROUPS):
        a, b, c, d = sel_rows[g * EXPERTS_PER_GROUP:(g + 1) * EXPERTS_PER_GROUP]
        hi1, lo1 = jnp.maximum(a, b), jnp.minimum(a, b)
        hi2, lo2 = jnp.maximum(c, d), jnp.minimum(c, d)
        top1 = jnp.maximum(hi1, hi2)
        top2 = jnp.maximum(jnp.minimum(hi1, hi2), jnp.where(hi1 >= hi2, lo1, lo2))
        scores.append(top1 + top2)
    best_g = _argmax_first(scores)
    in_sel = [_select(best_g, [sel_rows[g * EXPERTS_PER_GROUP + k] for g in range(N_EXPERT_GROUPS)])
              for k in range(EXPERTS_PER_GROUP)]
    in_prob = [_select(best_g, [prob_rows[g * EXPERTS_PER_GROUP + k] for g in range(N_EXPERT_GROUPS)])
               for k in range(EXPERTS_PER_GROUP)]
    i1 = _argmax_first(in_sel)
    i2 = _argmax_first([jnp.where(i1 == k, -jnp.inf, in_sel[k]) for k in range(EXPERTS_PER_GROUP)])
    w1 = _select(i1, in_prob)
    w2 = _select(i2, in_prob)
    tot = w1 + w2
    base = best_g * EXPERTS_PER_GROUP
    return (base + i1, base + i2), (w1 / tot, w2 / tot)


def _out_kernel(mg_ref, x_ref, mod_ref, g_ref, w_ref, rw_ref, rb_ref, xo_ref, h_ref, e_ref, gt_ref, acc0, acc1):
    s = pl.program_id(0)

    @pl.when(s == 0)
    def _():
        acc1[...] = jnp.zeros(acc1.shape, acc1.dtype)

    def step(acc_w, acc_r):
        gate1 = mod_ref[0, 0, 2:3, :]
        shift2 = mod_ref[0, 0, 3:4, :]
        scale2 = 1.0 + mod_ref[0, 0, 4:5, :]
        g = g_ref[0]
        acc_w[...] = jnp.dot(mg_ref[...], w_ref[0], preferred_element_type=F32)
        for c in range(x_ref.shape[0] // ROW_CHUNK):
            rows = slice(c * ROW_CHUNK, (c + 1) * ROW_CHUNK)
            xn = x_ref[rows, :] + gate1 * acc_r[rows, :]
            xo_ref[rows, :] = xn
            h_ref[rows, :] = (_rms_scale(xn) * g * scale2 + shift2).astype(BF16)
        logits_t = lax.dot_general(rw_ref[...], h_ref[...], _NT, preferred_element_type=F32)
        experts, gates = _route(logits_t, rb_ref[...])
        for k in range(TOP_K):
            e_ref[k:k + 1, :] = experts[k]
            gt_ref[k:k + 1, :] = gates[k]

    pl.when(s % 2 == 0)(lambda: step(acc0, acc1))
    pl.when(s % 2 == 1)(lambda: step(acc1, acc0))


def _output_projection(merged, x, mod, norm_g, w_out, router_wt, router_b, l, cond_of_block, tm):
    T, D = x.shape
    E = router_wt.shape[0]
    n = T // tm
    cur = lambda s: jnp.minimum(s, n - 1)
    prev = lambda s: jnp.maximum(s - 1, 0)
    return pl.pallas_call(
        _out_kernel,
        grid=(n + 1,),
        in_specs=[pl.BlockSpec((tm, D), lambda s: (cur(s), 0)),
                  pl.BlockSpec((tm, D), lambda s: (prev(s), 0)),
                  pl.BlockSpec((1, 1, N_MOD, D), lambda s: (l, cond_of_block(prev(s)), 0, 0)),
                  pl.BlockSpec((1, 1, D), lambda s: (l, 0, 0)),
                  pl.BlockSpec((1, D, D), lambda s: (l, 0, 0)),
                  pl.BlockSpec((E, D), lambda s: (0, 0)),
                  pl.BlockSpec((E, 1), lambda s: (0, 0))],
        out_specs=[pl.BlockSpec((tm, D), lambda s: (prev(s), 0)),
                   pl.BlockSpec((tm, D), lambda s: (prev(s), 0)),
                   pl.BlockSpec((TOP_K, tm), lambda s: (0, prev(s))),
                   pl.BlockSpec((TOP_K, tm), lambda s: (0, prev(s)))],
        out_shape=[jax.ShapeDtypeStruct((T, D), F32),
                   jax.ShapeDtypeStruct((T, D), BF16),
                   jax.ShapeDtypeStruct((TOP_K, T), I32),
                   jax.ShapeDtypeStruct((TOP_K, T), F32)],
        scratch_shapes=[pltpu.VMEM((tm, D), F32), pltpu.VMEM((tm, D), F32)],
        compiler_params=_params(1, 56),
        name="output_projection_norm2_router",
    )(merged, x, mod, norm_g.reshape(-1, 1, D), w_out, router_wt, router_b.reshape(E, 1))


def _moe_kernel(blk_ref, exp_ref, lo_ref, hi_ref, x_ref, wg_ref, wu_ref, wd_ref, o_ref):
    del exp_ref
    w = pl.program_id(0)
    lo, hi = lo_ref[w], hi_ref[w]
    row0 = blk_ref[w] * MOE_BLOCK

    @pl.when(lo == row0)
    def _():
        o_ref[...] = jnp.zeros(o_ref.shape, o_ref.dtype)

    @pl.when(hi > lo)
    def _():
        x = x_ref[...]
        g = jnp.dot(x, wg_ref[0], preferred_element_type=F32)
        u = jnp.dot(x, wu_ref[0], preferred_element_type=F32)
        hid = (g * _sigmoid(g)) * u
        y = jnp.dot(hid.astype(BF16), wd_ref[0], preferred_element_type=F32)
        row = row0 + lax.broadcasted_iota(I32, (MOE_BLOCK, 1), 0)
        o_ref[...] += jnp.where((row >= lo) & (row < hi), y, 0.0).astype(o_ref.dtype)


def _moe_experts(xs, items, w_gate, w_up, w_down, l):
    A, D = xs.shape
    E, DE = w_gate.shape[1], w_gate.shape[3]
    w_gate = w_gate.reshape(-1, D, DE)
    w_up = w_up.reshape(-1, D, DE)
    w_down = w_down.reshape(-1, DE, D)
    n_items = items[0].shape[0]
    grid_spec = pltpu.PrefetchScalarGridSpec(
        num_scalar_prefetch=4,
        grid=(n_items,),
        in_specs=[pl.BlockSpec((MOE_BLOCK, D), lambda w, blk, ex, lo, hi: (blk[w], 0)),
                  pl.BlockSpec((1, D, DE), lambda w, blk, ex, lo, hi: (l * E + ex[w], 0, 0)),
                  pl.BlockSpec((1, D, DE), lambda w, blk, ex, lo, hi: (l * E + ex[w], 0, 0)),
                  pl.BlockSpec((1, DE, D), lambda w, blk, ex, lo, hi: (l * E + ex[w], 0, 0))],
        out_specs=pl.BlockSpec((MOE_BLOCK, D), lambda w, blk, ex, lo, hi: (blk[w], 0)),
    )
    return pl.pallas_call(
        _moe_kernel,
        grid_spec=grid_spec,
        out_shape=jax.ShapeDtypeStruct((A, D), BF16),
        compiler_params=_params(1, 48),
        name="moe_experts",
    )(*items, xs, w_gate, w_up, w_down)


def _dispatch_plan(experts, T):
    A = TOP_K * T
    nb = A // MOE_BLOCK
    n_items = nb + N_EXPERTS - 1
    e_flat = experts.reshape(A)
    iota = jnp.arange(A, dtype=I32)
    _, order = lax.sort((e_flat, iota), num_keys=1, is_stable=True)
    _, rank = lax.sort((order, iota), num_keys=1)
    eid = jnp.arange(N_EXPERTS, dtype=I32)
    counts = jnp.sum((e_flat[:, None] == eid[None, :]).astype(I32), axis=0)
    ends = jnp.cumsum(counts)
    starts = ends - counts
    first_blk = starts // MOE_BLOCK
    n_blk = jnp.where(counts > 0, (ends - 1) // MOE_BLOCK - first_blk + 1, 0)
    item_end = jnp.cumsum(n_blk)
    item_start = item_end - n_blk
    w = jnp.arange(n_items, dtype=I32)
    valid = w < item_end[-1]
    ex = jnp.minimum(jnp.sum((w[:, None] >= item_end[None, :]).astype(I32), axis=1), N_EXPERTS - 1)
    pick = lambda tbl: jnp.sum(jnp.where(ex[:, None] == eid[None, :], tbl[None, :], 0), axis=1)
    blk = jnp.where(valid, pick(first_blk) + w - pick(item_start), nb - 1)
    lo = jnp.where(valid, jnp.maximum(pick(starts), blk * MOE_BLOCK), 0)
    hi = jnp.where(valid, jnp.minimum(pick(ends), (blk + 1) * MOE_BLOCK), 0)
    ex = jnp.where(valid, ex, jnp.max(jnp.where(counts > 0, eid, 0)))
    return order % T, rank, (blk.astype(I32), ex.astype(I32), lo.astype(I32), hi.astype(I32))


def _combined(x_ref, y1_ref, y2_ref, g1_ref, g2_ref, gate2, rows):
    y = g1_ref[rows, :] * y1_ref[rows, :].astype(F32) + g2_ref[rows, :] * y2_ref[rows, :].astype(F32)
    return x_ref[rows, :] + gate2 * y


def _combine_kernel(x_ref, y1_ref, y2_ref, g1_ref, g2_ref, mod_ref, modn_ref, gn_ref, xo_ref, h_ref):
    gate2 = mod_ref[0, 0, 5:6, :]
    scale1 = 1.0 + modn_ref[0, 0, 1:2, :]
    shift1 = modn_ref[0, 0, 0:1, :]
    g = gn_ref[0]

    def chunk(rows):
        xn = _combined(x_ref, y1_ref, y2_ref, g1_ref, g2_ref, gate2, rows)
        xo_ref[rows, :] = xn
        h_ref[rows, :] = (_rms_scale(xn) * g * scale1 + shift1).astype(BF16)

    _for_chunks(x_ref.shape[0], ROW_CHUNK, chunk)


def _combine_final_kernel(x_ref, y1_ref, y2_ref, g1_ref, g2_ref, mod_ref, gf_ref, yp_ref, ys_ref, *, ncb):
    i = pl.program_id(0)
    gate2 = mod_ref[0, 0, 5:6, :]
    g = gf_ref[...]

    def run(dst_ref):
        def chunk(rows):
            xn = _combined(x_ref, y1_ref, y2_ref, g1_ref, g2_ref, gate2, rows)
            dst_ref[rows, :] = _rms_scale(xn) * g

        _for_chunks(x_ref.shape[0], ROW_CHUNK, chunk)

    pl.when(i < ncb)(lambda: run(yp_ref))
    pl.when(i >= ncb)(lambda: run(ys_ref))


def _combine(x, y1, y2, g1, g2, mod, norm1_g, l, cond_of_block, tm):
    T, D = x.shape
    row = pl.BlockSpec((tm, D), lambda i: (i, 0))
    col = pl.BlockSpec((tm, 1), lambda i: (i, 0))
    return pl.pallas_call(
        _combine_kernel,
        grid=(T // tm,),
        in_specs=[row, row, row, col, col,
                  pl.BlockSpec((1, 1, N_MOD, D), lambda i: (l, cond_of_block(i), 0, 0)),
                  pl.BlockSpec((1, 1, N_MOD, D), lambda i: (l + 1, cond_of_block(i), 0, 0)),
                  pl.BlockSpec((1, 1, D), lambda i: (l + 1, 0, 0))],
        out_specs=[row, row],
        out_shape=[jax.ShapeDtypeStruct((T, D), F32), jax.ShapeDtypeStruct((T, D), BF16)],
        compiler_params=_params(1, 52),
        name="moe_combine_residual_norm1",
    )(x, y1, y2, g1, g2, mod, mod, norm1_g.reshape(-1, 1, D))


def _combine_final(x, y1, y2, g1, g2, mod, final_g, l, nc, cond_of_block, tm):
    T, D = x.shape
    ncb = nc // tm
    row = pl.BlockSpec((tm, D), lambda i: (i, 0))
    col = pl.BlockSpec((tm, 1), lambda i: (i, 0))
    return pl.pallas_call(
        functools.partial(_combine_final_kernel, ncb=ncb),
        grid=(T // tm,),
        in_specs=[row, row, row, col, col,
                  pl.BlockSpec((1, 1, N_MOD, D), lambda i: (l, cond_of_block(i), 0, 0)),
                  pl.BlockSpec((1, D), lambda i: (0, 0))],
        out_specs=[pl.BlockSpec((tm, D), lambda i: (jnp.minimum(i, ncb - 1), 0)),
                   pl.BlockSpec((tm, D), lambda i: (jnp.maximum(i - ncb, 0), 0))],
        out_shape=[jax.ShapeDtypeStruct((nc, D), F32), jax.ShapeDtypeStruct((T - nc, D), F32)],
        compiler_params=_params(1, 52),
        name="moe_combine_residual_final_norm",
    )(x, y1, y2, g1, g2, mod, final_g.reshape(1, D))


def kernel(x_prompt, x_sample, cache_k, cache_v, c, c_ctx, w_ada, b_ada, norm1_g, w_in, gmlp_g, w_s, b_s, rpb,
           w_proj_a, w_proj_b, w_out, norm2_g, router_w, router_b, w_gate, w_up, w_down, final_g):
    batch, seq, D = x_prompt.shape
    dec_batch, dec_seq, _ = x_sample.shape
    depth = w_in.shape[0]
    heads, dh = cache_k.shape[3], cache_k.shape[4]
    past = cache_k.shape[2]
    nc, nl = batch * seq, dec_batch * dec_seq
    T = nc + nl
    assert heads * dh == D and gmlp_g.shape[1] == D and w_in.shape[2] == 7 * D
    assert 1 + dec_batch <= COND_ROWS
    assert router_w.shape[1] == N_EXPERTS
    assert seq % GMLP_CHUNK == 0 and dec_seq % GMLP_CHUNK == 0 and dec_seq % GRID_W == 0
    assert nc % dec_seq == 0 and (TOP_K * T) % MOE_BLOCK == 0

    tm_big = 1024 if (nc % 1024 == 0 and dec_seq % 1024 == 0) else GMLP_CHUNK
    tm_mid = 512 if (nc % 512 == 0 and dec_seq % 512 == 0) else GMLP_CHUNK

    def cond_of(tm):
        ncb, per = nc // tm, dec_seq // tm
        return lambda i: jnp.where(i < ncb, 0, 1 + (i - ncb) // per)

    cond = jnp.zeros((COND_ROWS, D), F32).at[0].set(c_ctx).at[1:1 + dec_batch].set(c)
    mod = _modulation_all(cond, w_ada, b_ada)

    w_s_b = w_s.astype(BF16)
    w_pa_b = w_proj_a.astype(BF16)
    w_pb_b = w_proj_b.astype(BF16)
    w_out_b = w_out.astype(BF16)
    w_gate_b = w_gate.astype(BF16)
    w_up_b = w_up.astype(BF16)
    w_down_b = w_down.astype(BF16)
    kc_all = cache_k.astype(BF16).reshape(dec_batch, depth, past, D)
    vc_all = cache_v.astype(BF16).reshape(dec_batch, depth, past, D)
    router_wt = router_w.T.astype(BF16)
    na_tables = _na_tables(rpb, dec_seq // GRID_W)

    x, h = _prologue(x_prompt.reshape(nc, D), x_sample.reshape(nl, D), mod, norm1_g, cond_of(tm_mid), tm_mid)
    new_k = jnp.zeros((batch, depth, seq, D), F32)
    new_v = jnp.zeros((batch, depth, seq, D), F32)
    for l in range(depth):
        proj, new_k, new_v = _input_projection(h, w_in, new_k, new_v, l, tm_big, dh ** -0.5 * LOG2E)
        ya = _gmlp(proj, gmlp_g, w_s_b, b_s, l, D, tm_mid)
        yb_ctx = _context_attention(proj, batch, seq, D, heads, dh)
        yb_lat = _neighbourhood_attention(proj, kc_all[:, l], vc_all[:, l], na_tables, l, nc, dec_batch,
                                          dec_seq, D, heads, dh)
        merged = _merge(ya, yb_ctx, yb_lat, proj, w_pa_b, w_pb_b, l, D, tm_big)
        x, h2, experts, gates = _output_projection(merged, x, mod, norm2_g, w_out_b, router_wt, router_b, l,
                                                   cond_of(tm_mid), tm_mid)
        tok_sorted, rank, items = _dispatch_plan(experts, T)
        yexp = _moe_experts(h2[tok_sorted], items, w_gate_b, w_up_b, w_down_b, l)
        y1, y2, g1, g2 = yexp[rank[:T]], yexp[rank[T:]], gates[0][:, None], gates[1][:, None]
        if l + 1 < depth:
            x, h = _combine(x, y1, y2, g1, g2, mod, norm1_g, l, cond_of(tm_mid), tm_mid)
        else:
            y_prompt, y_sample = _combine_final(x, y1, y2, g1, g2, mod, final_g, l, nc, cond_of(tm_mid), tm_mid)

    return (y_prompt.reshape(batch, seq, D), y_sample.reshape(dec_batch, dec_seq, D),
            new_k.reshape(batch, depth, seq, heads, dh), new_v.reshape(batch, depth, seq, heads, dh))
```

```python
import functools

import numpy as np
import jax
import jax.numpy as jnp
from jax import lax
from jax.experimental import pallas as pl
from jax.experimental.pallas import tpu as pltpu

F32 = jnp.float32
BF16 = jnp.bfloat16
I32 = jnp.int32

EPS = 1e-6
N_MOD = 6
GRID_W = 64
WIN_ROWS = 8
WIN_COLS = 16
GMLP_CHUNK = 128
GMLP_GROUPS = 8
N_EXPERTS = 16
N_EXPERT_GROUPS = 4
EXPERTS_PER_GROUP = N_EXPERTS // N_EXPERT_GROUPS
TOP_K = 2
MOE_BLOCK = 256
COND_ROWS = 8
ROW_CHUNK = 64
CAST_CHUNK = 256
NA_QROWS = 4
NA_KROWS = 12
NA_KBLOCK = 256
NA_ONES_ROWS = 16
NA_HEADS_PER_STEP = 4
NA_BLOCKS_PER_STEP = 1
LOG2E = 1.4426950408889634
MIB = 1024 * 1024

_NT = (((1,), (1,)), ((), ()))


def _tile(n, pref):
    return pref if n % pref == 0 else n


def _params(n_axes, vmem_mib):
    return pltpu.CompilerParams(dimension_semantics=("arbitrary",) * n_axes,
                                vmem_limit_bytes=vmem_mib * MIB)


def _for_chunks(n, c, fn):
    c = min(c, n)

    def body(t, carry):
        fn(pl.ds(pl.multiple_of(t * c, c), c))
        return carry

    lax.fori_loop(0, n // c, body, 0)


def _sigmoid(x):
    return 1.0 / (1.0 + jnp.exp(-x))


def _gelu_tanh(x):
    return x * (0.5 * (1.0 + jnp.tanh(0.7978845608028654 * (x + 0.044715 * (x * x * x)))))


def _rms_scale(x):
    return x * lax.rsqrt(jnp.mean(x * x, axis=-1, keepdims=True) + EPS)


def _ada_kernel(c_ref, w_ref, b_ref, o_ref):
    c = c_ref[...]
    s = (c * _sigmoid(c)).astype(BF16)
    o_ref[0] = jnp.dot(s, w_ref[0].astype(BF16), preferred_element_type=F32) + b_ref[0]


def _modulation_all(cond, w_ada, b_ada):
    L, D, N = w_ada.shape
    tn = _tile(N, 1024)
    out = pl.pallas_call(
        _ada_kernel,
        grid=(L, N // tn),
        in_specs=[pl.BlockSpec((COND_ROWS, D), lambda l, j: (0, 0)),
                  pl.BlockSpec((1, D, tn), lambda l, j: (l, 0, j)),
                  pl.BlockSpec((1, 1, tn), lambda l, j: (l, 0, j))],
        out_specs=pl.BlockSpec((1, COND_ROWS, tn), lambda l, j: (l, 0, j)),
        out_shape=jax.ShapeDtypeStruct((L, COND_ROWS, N), F32),
        compiler_params=_params(2, 40),
        name="ada_modulation",
    )(cond, w_ada, b_ada.reshape(L, 1, N))
    return out.reshape(L, COND_ROWS, N_MOD, D)


def _prologue_kernel(xp_ref, xs_ref, mod_ref, g_ref, x_ref, h_ref, *, ncb):
    i = pl.program_id(0)
    g = g_ref[0]
    scale1 = 1.0 + mod_ref[0, 0, 1:2, :]
    shift1 = mod_ref[0, 0, 0:1, :]

    def run(src_ref):
        def chunk(rows):
            x = src_ref[rows, :]
            x_ref[rows, :] = x
            h_ref[rows, :] = (_rms_scale(x) * g * scale1 + shift1).astype(BF16)

        _for_chunks(x_ref.shape[0], ROW_CHUNK, chunk)

    pl.when(i < ncb)(lambda: run(xp_ref))
    pl.when(i >= ncb)(lambda: run(xs_ref))


def _prologue(xp, xs, mod, norm_g, cond_of_block, tm):
    nc, D = xp.shape
    T = nc + xs.shape[0]
    ncb = nc // tm
    row = pl.BlockSpec((tm, D), lambda i: (i, 0))
    return pl.pallas_call(
        functools.partial(_prologue_kernel, ncb=ncb),
        grid=(T // tm,),
        in_specs=[pl.BlockSpec((tm, D), lambda i: (jnp.minimum(i, ncb - 1), 0)),
                  pl.BlockSpec((tm, D), lambda i: (jnp.maximum(i - ncb, 0), 0)),
                  pl.BlockSpec((1, 1, N_MOD, D), lambda i: (0, cond_of_block(i), 0, 0)),
                  pl.BlockSpec((1, 1, D), lambda i: (0, 0, 0))],
        out_specs=[row, row],
        out_shape=[jax.ShapeDtypeStruct((T, D), F32), jax.ShapeDtypeStruct((T, D), BF16)],
        compiler_params=_params(1, 40),
        name="concat_norm1",
    )(xp, xs, mod, norm_g.reshape(-1, 1, D))


def _proj_kernel(h_ref, w_ref, nk_in, nv_in, o_ref, k_ref, v_ref, w_scr, acc0, acc1, *, ni, nt, ncb, jk0, nk, qscale):
    del nk_in, nv_in
    t = pl.program_id(0)
    tp = jnp.maximum(t - 1, 0)
    jp, ip = tp // ni, tp % ni

    @pl.when(t == 0)
    def _():
        acc1[...] = jnp.zeros(acc1.shape, acc1.dtype)

    @pl.when((t % ni == 0) & (t < nt))
    def _():
        def chunk(rows):
            w_scr[rows, :] = w_ref[0, rows, :].astype(BF16)

        _for_chunks(w_scr.shape[0], CAST_CHUNK, chunk)

    def put(dst_ref, acc_r):
        seq = dst_ref.shape[2]
        for b in range(dst_ref.shape[0]):
            dst_ref[b, 0] = acc_r[b * seq:(b + 1) * seq, :]

    def step(acc_w, acc_r):
        acc_w[...] = jnp.dot(h_ref[...], w_scr[...], preferred_element_type=F32)
        o_ref[...] = (acc_r[...] * jnp.where(jp < jk0, qscale, 1.0)).astype(BF16)
        ctx = (t > 0) & (ip < ncb)
        pl.when(ctx & (jp >= jk0) & (jp < jk0 + nk))(lambda: put(k_ref, acc_r))
        pl.when(ctx & (jp >= jk0 + nk) & (jp < jk0 + 2 * nk))(lambda: put(v_ref, acc_r))

    pl.when(t % 2 == 0)(lambda: step(acc0, acc1))
    pl.when(t % 2 == 1)(lambda: step(acc1, acc0))


def _input_projection(h, w_in, new_k, new_v, l, tm, qscale):
    T, D = h.shape
    N = w_in.shape[-1]
    batch, _, seq, _ = new_k.shape
    tn = _tile(D, 1024)
    ncb = batch * seq // tm
    jk0, nk = D // tn, D // tn
    ni, nt = T // tm, (T // tm) * (N // tn)
    kernel = functools.partial(_proj_kernel, ni=ni, nt=nt, ncb=ncb, jk0=jk0, nk=nk, qscale=qscale)
    cur = lambda t: jnp.minimum(t, nt - 1)
    prev = lambda t: jnp.maximum(t - 1, 0)

    def cache_block(j0):
        def index(t):
            j, i = prev(t) // ni, prev(t) % ni
            inside = (j >= j0) & (j < j0 + nk)
            row = jnp.where(j < j0, 0, jnp.where(inside, jnp.minimum(i, ncb - 1), ncb - 1))
            col = jnp.where(j < j0, 0, jnp.where(inside, j - j0, nk - 1))
            return (row, l, 0, col)
        return pl.BlockSpec((tm // seq, 1, seq, tn), index)

    return pl.pallas_call(
        kernel,
        grid=(nt + 1,),
        in_specs=[pl.BlockSpec((tm, D), lambda t: (cur(t) % ni, 0)),
                  pl.BlockSpec((1, D, tn), lambda t: (l, 0, cur(t) // ni)),
                  pl.BlockSpec(memory_space=pl.ANY),
                  pl.BlockSpec(memory_space=pl.ANY)],
        out_specs=[pl.BlockSpec((tm, tn), lambda t: (prev(t) % ni, prev(t) // ni)),
                   cache_block(jk0), cache_block(jk0 + nk)],
        out_shape=[jax.ShapeDtypeStruct((T, N), BF16),
                   jax.ShapeDtypeStruct(new_k.shape, F32),
                   jax.ShapeDtypeStruct(new_v.shape, F32)],
        scratch_shapes=[pltpu.VMEM((D, tn), BF16), pltpu.VMEM((tm, tn), F32), pltpu.VMEM((tm, tn), F32)],
        input_output_aliases={2: 1, 3: 2},
        compiler_params=_params(1, 60),
        name="input_projection",
    )(h, w_in, new_k, new_v)


def _gmlp_kernel(u_ref, gv_ref, g_ref, ws_ref, bs_ref, o_ref):
    gw = u_ref.shape[1] // GMLP_GROUPS
    g = g_ref[0]

    def chunk(rows):
        v = _rms_scale(_gelu_tanh(gv_ref[rows, :].astype(F32))) * g
        v = v.astype(BF16)
        for grp in range(GMLP_GROUPS):
            cols = slice(grp * gw, (grp + 1) * gw)
            s = jnp.dot(ws_ref[grp], v[:, cols], preferred_element_type=F32) + bs_ref[grp]
            u = _gelu_tanh(u_ref[rows, cols].astype(F32))
            o_ref[rows, cols] = (u * s).astype(BF16)

    _for_chunks(u_ref.shape[0], GMLP_CHUNK, chunk)


def _gmlp(proj, gmlp_g, w_s, b_s, l, D, tm):
    T = proj.shape[0]
    gw = D // GMLP_GROUPS
    bs = jnp.broadcast_to(b_s[l][:, :, None], (GMLP_GROUPS, GMLP_CHUNK, gw))
    return pl.pallas_call(
        _gmlp_kernel,
        grid=(T // tm,),
        in_specs=[pl.BlockSpec((tm, D), lambda i: (i, 3)),
                  pl.BlockSpec((tm, D), lambda i: (i, 4)),
                  pl.BlockSpec((1, 1, D), lambda i: (l, 0, 0)),
                  pl.BlockSpec((None, GMLP_GROUPS, GMLP_CHUNK, GMLP_CHUNK), lambda i: (l, 0, 0, 0)),
                  pl.BlockSpec((GMLP_GROUPS, GMLP_CHUNK, gw), lambda i: (0, 0, 0))],
        out_specs=pl.BlockSpec((tm, D), lambda i: (i, 0)),
        out_shape=jax.ShapeDtypeStruct((T, D), BF16),
        compiler_params=_params(1, 40),
        name="gmlp_spatial_gating",
    )(proj, proj, gmlp_g.reshape(-1, 1, D), w_s, bs)


def _ctx_attn_kernel(q_ref, k_ref, v_ref, o_ref, *, heads, dh):
    for h in range(heads):
        cols = slice(h * dh, (h + 1) * dh)
        s = lax.dot_general(q_ref[:, cols], k_ref[:, cols], _NT, preferred_element_type=F32)
        p = jnp.exp2(s - jnp.max(s, axis=-1, keepdims=True))
        denom = jnp.sum(p, axis=-1, keepdims=True)
        o = jnp.dot(p.astype(BF16), v_ref[:, cols], preferred_element_type=F32)
        o_ref[:, cols] = (o / denom).astype(BF16)


def _context_attention(proj, batch, seq, D, heads, dh):
    return pl.pallas_call(
        functools.partial(_ctx_attn_kernel, heads=heads, dh=dh),
        grid=(batch,),
        in_specs=[pl.BlockSpec((seq, D), lambda b: (b, 0)),
                  pl.BlockSpec((seq, D), lambda b: (b, 1)),
                  pl.BlockSpec((seq, D), lambda b: (b, 2))],
        out_specs=pl.BlockSpec((seq, D), lambda b: (b, 0)),
        out_shape=jax.ShapeDtypeStruct((batch * seq, D), BF16),
        compiler_params=_params(1, 32),
        name="context_attention",
    )(proj, proj, proj)


def _na_kernel(q_ref, k_ref, v_ref, kc_ref, vc_ref, *rest, hps, dh, rows):
    um_refs, (o_ref, vt_scr, vct_scr) = rest[:NA_BLOCKS_PER_STEP], rest[NA_BLOCKS_PER_STEP:]
    m = pl.program_id(2)
    win = NA_KROWS * GRID_W
    nkb = win // NA_KBLOCK
    hr = dh + NA_ONES_ROWS

    @pl.when(m == 0)
    def _():
        def fill(dst, vt):
            for h in range(hps):
                dst[h * hr:h * hr + dh, :] = vt[h * dh:(h + 1) * dh, :]
                dst[h * hr + dh:(h + 1) * hr, :] = jnp.ones((NA_ONES_ROWS, vt.shape[1]), BF16)

        for kb in range(v_ref.shape[0] // NA_KBLOCK):
            fill(vt_scr.at[kb], v_ref[kb * NA_KBLOCK:(kb + 1) * NA_KBLOCK, :].astype(F32).T.astype(BF16))
        fill(vct_scr, vc_ref[0].astype(F32).T.astype(BF16))

    qn = NA_QROWS * GRID_W
    units = [(sub, h) for sub in range(len(um_refs)) for h in range(hps)]
    kb0 = []
    scores = []
    for sub, um_ref in enumerate(um_refs):
        ws = jnp.clip((m * len(um_refs) + sub) * NA_QROWS - WIN_ROWS // 2, 0, rows - NA_KROWS)
        krows = pl.ds(pl.multiple_of(ws * GRID_W, NA_KBLOCK), win)
        kb0.append(ws // (NA_KBLOCK // GRID_W))
        for h in range(hps):
            cols = slice(h * dh, (h + 1) * dh)
            q = q_ref[sub * qn:(sub + 1) * qn, cols]
            s_loc = lax.dot_general(k_ref[krows, cols], q, _NT, preferred_element_type=F32) + um_ref[0, 0, h]
            s_ctx = lax.dot_general(kc_ref[0, :, cols], q, _NT, preferred_element_type=F32)
            scores.append((s_loc, s_ctx))
    probs = []
    for s_loc, s_ctx in scores:
        mx = jnp.maximum(jnp.max(s_loc, axis=0, keepdims=True), jnp.max(s_ctx, axis=0, keepdims=True))
        probs.append((jnp.exp2(s_loc - mx).astype(BF16), jnp.exp2(s_ctx - mx).astype(BF16)))
    for (sub, h), (p_loc, p_ctx) in zip(units, probs):
        hrows = slice(h * hr, (h + 1) * hr)
        ot = jnp.dot(vct_scr[hrows, :], p_ctx, preferred_element_type=F32)
        for t in range(nkb):
            ot = ot + jnp.dot(vt_scr[kb0[sub] + t, hrows, :], p_loc[t * NA_KBLOCK:(t + 1) * NA_KBLOCK, :],
                              preferred_element_type=F32)
        o_ref[sub * qn:(sub + 1) * qn, h * dh:(h + 1) * dh] = (ot[:dh] / ot[dh:dh + 1]).T.astype(BF16)


def _na_tables(rpb, rows):
    L, H = rpb.shape[:2]
    kh = WIN_ROWS
    col = np.arange(GRID_W)
    cs = np.clip(col - WIN_COLS // 2, 0, GRID_W - WIN_COLS)
    col_valid = (col[:, None] >= cs[None, :]) & (col[:, None] < cs[None, :] + WIN_COLS)
    n_rho = NA_KROWS + WIN_ROWS
    n_dr = 2 * WIN_ROWS - 1
    j = np.arange(NA_QROWS)
    flipped = jnp.pad(jnp.flip(rpb.astype(F32), axis=-1), ((0, 0), (0, 0), (0, 0), (GRID_W, GRID_W)))
    t1 = jnp.stack([flipped[..., GRID_W + WIN_COLS - 1 - kc:2 * GRID_W + WIN_COLS - 1 - kc] for kc in range(GRID_W)],
                   axis=3)
    t1 = jnp.where(col_valid, t1, -jnp.inf)
    lanes = []
    for jj in range(NA_QROWS):
        dr = np.clip(np.arange(n_rho) - WIN_ROWS - jj + WIN_ROWS - 1, 0, n_dr - 1)
        lo, hi = int((dr == 0).sum()) - 1, int((dr == n_dr - 1).sum()) - 1
        assert lo + n_dr + hi == n_rho
        lanes.append(jnp.concatenate([t1[:, :, :1]] * lo + [t1] + [t1[:, :, -1:]] * hi, axis=2))
    u = jnp.concatenate(lanes, axis=-1).reshape(L, H, n_rho * GRID_W, NA_QROWS * GRID_W)

    def band(r0, ws):
        rs = np.clip(r0 + j - kh // 2, 0, rows - kh)
        key_row = ws + np.arange(NA_KROWS)
        ok = (key_row[:, None] >= rs[None, :]) & (key_row[:, None] < rs[None, :] + kh)
        mk = np.where(ok, 0.0, -np.inf).astype(np.float32)
        return np.broadcast_to(mk[:, None, :, None], (NA_KROWS, GRID_W, NA_QROWS, GRID_W)).reshape(
            NA_KROWS * GRID_W, NA_QROWS * GRID_W)

    win = NA_KROWS * GRID_W
    kinds = []
    for r0, ws in ((0, 0), (NA_QROWS, 0), (rows - NA_QROWS, rows - NA_KROWS)):
        start = (ws - r0 + WIN_ROWS) * GRID_W
        kinds.append(u[:, :, start:start + win] * LOG2E + jnp.asarray(band(r0, ws)))
    return jnp.stack(kinds, axis=1)


def _neighbourhood_attention(proj, kc, vc, um, l, nc, dec_batch, dec_seq, D, heads, dh):
    rows = dec_seq // GRID_W
    nsub = NA_BLOCKS_PER_STEP
    assert rows % (NA_QROWS * nsub) == 0 and rows >= NA_KROWS
    hps = min(NA_HEADS_PER_STEP, heads)
    qn = nsub * NA_QROWS * GRID_W
    cw = hps * dh
    n_hg = heads // hps
    nblk = rows // NA_QROWS
    nstep = nblk // nsub
    q0 = nc // qn
    k0 = nc // dec_seq
    ncols = D // cw
    past = kc.shape[1]
    hr = dh + NA_ONES_ROWS
    kind = lambda blk: jnp.where(blk == 0, 0, jnp.where(blk == nblk - 1, 2, 1))

    def table_spec(sub):
        return pl.BlockSpec((1, 1, hps) + um.shape[3:], lambda b, g, m: (l, kind(m * nsub + sub), g, 0, 0))

    kernel = functools.partial(_na_kernel, hps=hps, dh=dh, rows=rows)
    return pl.pallas_call(
        kernel,
        grid=(dec_batch, n_hg, nstep),
        in_specs=[pl.BlockSpec((qn, cw), lambda b, g, m: (q0 + b * nstep + m, g)),
                  pl.BlockSpec((dec_seq, cw), lambda b, g, m: (k0 + b, ncols + g)),
                  pl.BlockSpec((dec_seq, cw), lambda b, g, m: (k0 + b, 2 * ncols + g)),
                  pl.BlockSpec((1, past, cw), lambda b, g, m: (b, 0, g)),
                  pl.BlockSpec((1, past, cw), lambda b, g, m: (b, 0, g))] + [table_spec(s) for s in range(nsub)],
        out_specs=pl.BlockSpec((qn, cw), lambda b, g, m: (b * nstep + m, g)),
        out_shape=jax.ShapeDtypeStruct((dec_batch * dec_seq, D), BF16),
        scratch_shapes=[pltpu.VMEM((dec_seq // NA_KBLOCK, hps * hr, NA_KBLOCK), BF16),
                        pltpu.VMEM((hps * hr, past), BF16)],
        compiler_params=_params(3, 56),
        name="neighbourhood_attention",
    )(proj, proj, proj, kc, vc, *([um] * nsub))


def _merge_kernel(ya_ref, ybc_ref, ybl_ref, ga_ref, gb_ref, wa_ref, wb_ref, o_ref, *, ncb):
    yb = jnp.where(pl.program_id(0) < ncb, ybc_ref[...], ybl_ref[...])
    a = jnp.dot(ya_ref[...], wa_ref[0], preferred_element_type=F32)
    b = jnp.dot(yb, wb_ref[0], preferred_element_type=F32)
    o = _sigmoid(ga_ref[...].astype(F32)) * a + _sigmoid(gb_ref[...].astype(F32)) * b
    o_ref[...] = o.astype(BF16)


def _merge(ya, yb_ctx, yb_lat, proj, w_pa, w_pb, l, D, tm):
    T = ya.shape[0]
    tn = _tile(D, 512)
    nj = D // tn
    ncb = yb_ctx.shape[0] // tm
    return pl.pallas_call(
        functools.partial(_merge_kernel, ncb=ncb),
        grid=(T // tm, nj),
        in_specs=[pl.BlockSpec((tm, D), lambda i, j: (i, 0)),
                  pl.BlockSpec((tm, D), lambda i, j: (jnp.minimum(i, ncb - 1), 0)),
                  pl.BlockSpec((tm, D), lambda i, j: (jnp.maximum(i - ncb, 0), 0)),
                  pl.BlockSpec((tm, tn), lambda i, j: (i, 5 * nj + j)),
                  pl.BlockSpec((tm, tn), lambda i, j: (i, 6 * nj + j)),
                  pl.BlockSpec((1, D, tn), lambda i, j: (l, 0, j)),
                  pl.BlockSpec((1, D, tn), lambda i, j: (l, 0, j))],
        out_specs=pl.BlockSpec((tm, tn), lambda i, j: (i, j)),
        out_shape=jax.ShapeDtypeStruct((T, D), BF16),
        compiler_params=_params(2, 56),
        name="gated_branch_merge",
    )(ya, yb_ctx, yb_lat, proj, proj, w_pa, w_pb)


def _argmax_first(vals):
    best = vals[0]
    idx = jnp.zeros(best.shape, I32)
    for k in range(1, len(vals)):
        upd = vals[k] > best
        idx = jnp.where(upd, k, idx)
        best = jnp.where(upd, vals[k], best)
    return idx


def _select(idx, vals):
    out = vals[-1]
    for k in range(len(vals) - 2, -1, -1):
        out = jnp.where(idx == k, vals[k], out)
    return out


def _route(logits_t, router_b):
    m = jnp.max(logits_t, axis=0, keepdims=True)
    ex = jnp.exp(logits_t - m)
    probs = ex / jnp.sum(ex, axis=0, keepdims=True)
    sel = probs + router_b
    sel_rows = [sel[e:e + 1, :] for e in range(N_EXPERTS)]
    prob_rows = [probs[e:e + 1, :] for e in range(N_EXPERTS)]
    scores = []
    for g in range(N_EXPERT_GROUPS):
        a, b, c, d = sel_rows[g * EXPERTS_PER_GROUP:(g + 1) * EXPERTS_PER_GROUP]
        hi1, lo1 = jnp.maximum(a, b), jnp.minimum(a, b)
        hi2, lo2 = jnp.maximum(c, d), jnp.minimum(c, d)
        top1 = jnp.maximum(hi1, hi2)
        top2 = jnp.maximum(jnp.minimum(hi1, hi2), jnp.where(hi1 >= hi2, lo1, lo2))
        scores.append(top1 + top2)
    best_g = _argmax_first(scores)
    in_sel = [_select(best_g, [sel_rows[g * EXPERTS_PER_GROUP + k] for g in range(N_EXPERT_GROUPS)])
              for k in range(EXPERTS_PER_GROUP)]
    in_prob = [_select(best_g, [prob_rows[g * EXPERTS_PER_GROUP + k] for g in range(N_EXPERT_GROUPS)])
               for k in range(EXPERTS_PER_GROUP)]
    i1 = _argmax_first(in_sel)
    i2 = _argmax_first([jnp.where(i1 == k, -jnp.inf, in_sel[k]) for k in range(EXPERTS_PER_GROUP)])
    w1 = _select(i1, in_prob)
    w2 = _select(i2, in_prob)
    tot = w1 + w2
    base = best_g * EXPERTS_PER_GROUP
    return (base + i1, base + i2), (w1 / tot, w2 / tot)


def _out_kernel(mg_ref, x_ref, mod_ref, g_ref, w_ref, rw_ref, rb_ref, xo_ref, h_ref, e_ref, gt_ref, acc0, acc1):
    s = pl.program_id(0)

    @pl.when(s == 0)
    def _():
        acc1[...] = jnp.zeros(acc1.shape, acc1.dtype)

    def step(acc_w, acc_r):
        gate1 = mod_ref[0, 0, 2:3, :]
        shift2 = mod_ref[0, 0, 3:4, :]
        scale2 = 1.0 + mod_ref[0, 0, 4:5, :]
        g = g_ref[0]
        acc_w[...] = jnp.dot(mg_ref[...], w_ref[0], preferred_element_type=F32)
        for c in range(x_ref.shape[0] // ROW_CHUNK):
            rows = slice(c * ROW_CHUNK, (c + 1) * ROW_CHUNK)
            xn = x_ref[rows, :] + gate1 * acc_r[rows, :]
            xo_ref[rows, :] = xn
            h_ref[rows, :] = (_rms_scale(xn) * g * scale2 + shift2).astype(BF16)
        logits_t = lax.dot_general(rw_ref[...], h_ref[...], _NT, preferred_element_type=F32)
        experts, gates = _route(logits_t, rb_ref[...])
        for k in range(TOP_K):
            e_ref[k:k + 1, :] = experts[k]
            gt_ref[k:k + 1, :] = gates[k]

    pl.when(s % 2 == 0)(lambda: step(acc0, acc1))
    pl.when(s % 2 == 1)(lambda: step(acc1, acc0))


def _output_projection(merged, x, mod, norm_g, w_out, router_wt, router_b, l, cond_of_block, tm):
    T, D = x.shape
    E = router_wt.shape[0]
    n = T // tm
    cur = lambda s: jnp.minimum(s, n - 1)
    prev = lambda s: jnp.maximum(s - 1, 0)
    return pl.pallas_call(
        _out_kernel,
        grid=(n + 1,),
        in_specs=[pl.BlockSpec((tm, D), lambda s: (cur(s), 0)),
                  pl.BlockSpec((tm, D), lambda s: (prev(s), 0)),
                  pl.BlockSpec((1, 1, N_MOD, D), lambda s: (l, cond_of_block(prev(s)), 0, 0)),
                  pl.BlockSpec((1, 1, D), lambda s: (l, 0, 0)),
                  pl.BlockSpec((1, D, D), lambda s: (l, 0, 0)),
                  pl.BlockSpec((E, D), lambda s: (0, 0)),
                  pl.BlockSpec((E, 1), lambda s: (0, 0))],
        out_specs=[pl.BlockSpec((tm, D), lambda s: (prev(s), 0)),
                   pl.BlockSpec((tm, D), lambda s: (prev(s), 0)),
                   pl.BlockSpec((TOP_K, tm), lambda s: (0, prev(s))),
                   pl.BlockSpec((TOP_K, tm), lambda s: (0, prev(s)))],
        out_shape=[jax.ShapeDtypeStruct((T, D), F32),
                   jax.ShapeDtypeStruct((T, D), BF16),
                   jax.ShapeDtypeStruct((TOP_K, T), I32),
                   jax.ShapeDtypeStruct((TOP_K, T), F32)],
        scratch_shapes=[pltpu.VMEM((tm, D), F32), pltpu.VMEM((tm, D), F32)],
        compiler_params=_params(1, 56),
        name="output_projection_norm2_router",
    )(merged, x, mod, norm_g.reshape(-1, 1, D), w_out, router_wt, router_b.reshape(E, 1))


def _moe_kernel(blk_ref, exp_ref, lo_ref, hi_ref, x_ref, wg_ref, wu_ref, wd_ref, o_ref):
    del exp_ref
    w = pl.program_id(0)
    lo, hi = lo_ref[w], hi_ref[w]
    row0 = blk_ref[w] * MOE_BLOCK

    @pl.when(lo == row0)
    def _():
        o_ref[...] = jnp.zeros(o_ref.shape, o_ref.dtype)

    @pl.when(hi > lo)
    def _():
        x = x_ref[...]
        g = jnp.dot(x, wg_ref[0], preferred_element_type=F32)
        u = jnp.dot(x, wu_ref[0], preferred_element_type=F32)
        hid = (g * _sigmoid(g)) * u
        y = jnp.dot(hid.astype(BF16), wd_ref[0], preferred_element_type=F32)
        row = row0 + lax.broadcasted_iota(I32, (MOE_BLOCK, 1), 0)
        o_ref[...] += jnp.where((row >= lo) & (row < hi), y, 0.0).astype(o_ref.dtype)


def _moe_experts(xs, items, w_gate, w_up, w_down, l):
    A, D = xs.shape
    E, DE = w_gate.shape[1], w_gate.shape[3]
    w_gate = w_gate.reshape(-1, D, DE)
    w_up = w_up.reshape(-1, D, DE)
    w_down = w_down.reshape(-1, DE, D)
    n_items = items[0].shape[0]
    grid_spec = pltpu.PrefetchScalarGridSpec(
        num_scalar_prefetch=4,
        grid=(n_items,),
        in_specs=[pl.BlockSpec((MOE_BLOCK, D), lambda w, blk, ex, lo, hi: (blk[w], 0)),
                  pl.BlockSpec((1, D, DE), lambda w, blk, ex, lo, hi: (l * E + ex[w], 0, 0)),
                  pl.BlockSpec((1, D, DE), lambda w, blk, ex, lo, hi: (l * E + ex[w], 0, 0)),
                  pl.BlockSpec((1, DE, D), lambda w, blk, ex, lo, hi: (l * E + ex[w], 0, 0))],
        out_specs=pl.BlockSpec((MOE_BLOCK, D), lambda w, blk, ex, lo, hi: (blk[w], 0)),
    )
    return pl.pallas_call(
        _moe_kernel,
        grid_spec=grid_spec,
        out_shape=jax.ShapeDtypeStruct((A, D), BF16),
        compiler_params=_params(1, 48),
        name="moe_experts",
    )(*items, xs, w_gate, w_up, w_down)


def _dispatch_plan(experts, T):
    A = TOP_K * T
    nb = A // MOE_BLOCK
    n_items = nb + N_EXPERTS - 1
    e_flat = experts.reshape(A)
    iota = jnp.arange(A, dtype=I32)
    _, order = lax.sort((e_flat, iota), num_keys=1, is_stable=True)
    _, rank = lax.sort((order, iota), num_keys=1)
    eid = jnp.arange(N_EXPERTS, dtype=I32)
    counts = jnp.sum((e_flat[:, None] == eid[None, :]).astype(I32), axis=0)
    ends = jnp.cumsum(counts)
    starts = ends - counts
    first_blk = starts // MOE_BLOCK
    n_blk = jnp.where(counts > 0, (ends - 1) // MOE_BLOCK - first_blk + 1, 0)
    item_end = jnp.cumsum(n_blk)
    item_start = item_end - n_blk
    w = jnp.arange(n_items, dtype=I32)
    valid = w < item_end[-1]
    ex = jnp.minimum(jnp.sum((w[:, None] >= item_end[None, :]).astype(I32), axis=1), N_EXPERTS - 1)
    pick = lambda tbl: jnp.sum(jnp.where(ex[:, None] == eid[None, :], tbl[None, :], 0), axis=1)
    blk = jnp.where(valid, pick(first_blk) + w - pick(item_start), nb - 1)
    lo = jnp.where(valid, jnp.maximum(pick(starts), blk * MOE_BLOCK), 0)
    hi = jnp.where(valid, jnp.minimum(pick(ends), (blk + 1) * MOE_BLOCK), 0)
    ex = jnp.where(valid, ex, jnp.max(jnp.where(counts > 0, eid, 0)))
    return order % T, rank, (blk.astype(I32), ex.astype(I32), lo.astype(I32), hi.astype(I32))


def _combined(x_ref, y1_ref, y2_ref, g1_ref, g2_ref, gate2, rows):
    y = g1_ref[rows, :] * y1_ref[rows, :].astype(F32) + g2_ref[rows, :] * y2_ref[rows, :].astype(F32)
    return x_ref[rows, :] + gate2 * y


def _combine_kernel(x_ref, y1_ref, y2_ref, g1_ref, g2_ref, mod_ref, modn_ref, gn_ref, xo_ref, h_ref):
    gate2 = mod_ref[0, 0, 5:6, :]
    scale1 = 1.0 + modn_ref[0, 0, 1:2, :]
    shift1 = modn_ref[0, 0, 0:1, :]
    g = gn_ref[0]

    def chunk(rows):
        xn = _combined(x_ref, y1_ref, y2_ref, g1_ref, g2_ref, gate2, rows)
        xo_ref[rows, :] = xn
        h_ref[rows, :] = (_rms_scale(xn) * g * scale1 + shift1).astype(BF16)

    _for_chunks(x_ref.shape[0], ROW_CHUNK, chunk)


def _combine_final_kernel(x_ref, y1_ref, y2_ref, g1_ref, g2_ref, mod_ref, gf_ref, yp_ref, ys_ref, *, ncb):
    i = pl.program_id(0)
    gate2 = mod_ref[0, 0, 5:6, :]
    g = gf_ref[...]

    def run(dst_ref):
        def chunk(rows):
            xn = _combined(x_ref, y1_ref, y2_ref, g1_ref, g2_ref, gate2, rows)
            dst_ref[rows, :] = _rms_scale(xn) * g

        _for_chunks(x_ref.shape[0], ROW_CHUNK, chunk)

    pl.when(i < ncb)(lambda: run(yp_ref))
    pl.when(i >= ncb)(lambda: run(ys_ref))


def _combine(x, y1, y2, g1, g2, mod, norm1_g, l, cond_of_block, tm):
    T, D = x.shape
    row = pl.BlockSpec((tm, D), lambda i: (i, 0))
    col = pl.BlockSpec((tm, 1), lambda i: (i, 0))
    return pl.pallas_call(
        _combine_kernel,
        grid=(T // tm,),
        in_specs=[row, row, row, col, col,
                  pl.BlockSpec((1, 1, N_MOD, D), lambda i: (l, cond_of_block(i), 0, 0)),
                  pl.BlockSpec((1, 1, N_MOD, D), lambda i: (l + 1, cond_of_block(i), 0, 0)),
                  pl.BlockSpec((1, 1, D), lambda i: (l + 1, 0, 0))],
        out_specs=[row, row],
        out_shape=[jax.ShapeDtypeStruct((T, D), F32), jax.ShapeDtypeStruct((T, D), BF16)],
        compiler_params=_params(1, 52),
        name="moe_combine_residual_norm1",
    )(x, y1, y2, g1, g2, mod, mod, norm1_g.reshape(-1, 1, D))


def _combine_final(x, y1, y2, g1, g2, mod, final_g, l, nc, cond_of_block, tm):
    T, D = x.shape
    ncb = nc // tm
    row = pl.BlockSpec((tm, D), lambda i: (i, 0))
    col = pl.BlockSpec((tm, 1), lambda i: (i, 0))
    return pl.pallas_call(
        functools.partial(_combine_final_kernel, ncb=ncb),
        grid=(T // tm,),
        in_specs=[row, row, row, col, col,
                  pl.BlockSpec((1, 1, N_MOD, D), lambda i: (l, cond_of_block(i), 0, 0)),
                  pl.BlockSpec((1, D), lambda i: (0, 0))],
        out_specs=[pl.BlockSpec((tm, D), lambda i: (jnp.minimum(i, ncb - 1), 0)),
                   pl.BlockSpec((tm, D), lambda i: (jnp.maximum(i - ncb, 0), 0))],
        out_shape=[jax.ShapeDtypeStruct((nc, D), F32), jax.ShapeDtypeStruct((T - nc, D), F32)],
        compiler_params=_params(1, 52),
        name="moe_combine_residual_final_norm",
    )(x, y1, y2, g1, g2, mod, final_g.reshape(1, D))


def kernel(x_prompt, x_sample, cache_k, cache_v, c, c_ctx, w_ada, b_ada, norm1_g, w_in, gmlp_g, w_s, b_s, rpb,
           w_proj_a, w_proj_b, w_out, norm2_g, router_w, router_b, w_gate, w_up, w_down, final_g):
    batch, seq, D = x_prompt.shape
    dec_batch, dec_seq, _ = x_sample.shape
    depth = w_in.shape[0]
    heads, dh = cache_k.shape[3], cache_k.shape[4]
    past = cache_k.shape[2]
    nc, nl = batch * seq, dec_batch * dec_seq
    T = nc + nl
    assert heads * dh == D and gmlp_g.shape[1] == D and w_in.shape[2] == 7 * D
    assert 1 + dec_batch <= COND_ROWS
    assert router_w.shape[1] == N_EXPERTS
    assert seq % GMLP_CHUNK == 0 and dec_seq % GMLP_CHUNK == 0 and dec_seq % GRID_W == 0
    assert nc % dec_seq == 0 and (TOP_K * T) % MOE_BLOCK == 0

    tm_big = 1024 if (nc % 1024 == 0 and dec_seq % 1024 == 0) else GMLP_CHUNK
    tm_mid = 512 if (nc % 512 == 0 and dec_seq % 512 == 0) else GMLP_CHUNK

    def cond_of(tm):
        ncb, per = nc // tm, dec_seq // tm
        return lambda i: jnp.where(i < ncb, 0, 1 + (i - ncb) // per)

    cond = jnp.zeros((COND_ROWS, D), F32).at[0].set(c_ctx).at[1:1 + dec_batch].set(c)
    mod = _modulation_all(cond, w_ada, b_ada)

    w_s_b = w_s.astype(BF16)
    w_pa_b = w_proj_a.astype(BF16)
    w_pb_b = w_proj_b.astype(BF16)
    w_out_b = w_out.astype(BF16)
    w_gate_b = w_gate.astype(BF16)
    w_up_b = w_up.astype(BF16)
    w_down_b = w_down.astype(BF16)
    kc_all = cache_k.astype(BF16).reshape(dec_batch, depth, past, D)
    vc_all = cache_v.astype(BF16).reshape(dec_batch, depth, past, D)
    router_wt = router_w.T.astype(BF16)
    na_tables = _na_tables(rpb, dec_seq // GRID_W)

    x, h = _prologue(x_prompt.reshape(nc, D), x_sample.reshape(nl, D), mod, norm1_g, cond_of(tm_mid), tm_mid)
    new_k = jnp.zeros((batch, depth, seq, D), F32)
    new_v = jnp.zeros((batch, depth, seq, D), F32)
    for l in range(depth):
        proj, new_k, new_v = _input_projection(h, w_in, new_k, new_v, l, tm_big, dh ** -0.5 * LOG2E)
        ya = _gmlp(proj, gmlp_g, w_s_b, b_s, l, D, tm_mid)
        yb_ctx = _context_attention(proj, batch, seq, D, heads, dh)
        yb_lat = _neighbourhood_attention(proj, kc_all[:, l], vc_all[:, l], na_tables, l, nc, dec_batch,
                                          dec_seq, D, heads, dh)
        merged = _merge(ya, yb_ctx, yb_lat, proj, w_pa_b, w_pb_b, l, D, tm_big)
        x, h2, experts, gates = _output_projection(merged, x, mod, norm2_g, w_out_b, router_wt, router_b, l,
                                                   cond_of(tm_mid), tm_mid)
        tok_sorted, rank, items = _dispatch_plan(experts, T)
        yexp = _moe_experts(h2[tok_sorted], items, w_gate_b, w_up_b, w_down_b, l)
        y1, y2, g1, g2 = yexp[rank[:T]], yexp[rank[T:]], gates[0][:, None], gates[1][:, None]
        if l + 1 < depth:
            x, h = _combine(x, y1, y2, g1, g2, mod, norm1_g, l, cond_of(tm_mid), tm_mid)
        else:
            y_prompt, y_sample = _combine_final(x, y1, y2, g1, g2, mod, final_g, l, nc, cond_of(tm_mid), tm_mid)

    return (y_prompt.reshape(batch, seq, D), y_sample.reshape(dec_batch, dec_seq, D),
            new_k.reshape(batch, depth, seq, heads, dh), new_v.reshape(batch, depth, seq, heads, dh))
```

```python
import functools

import numpy as np
import jax
import jax.numpy as jnp
from jax import lax
from jax.experimental import pallas as pl
from jax.experimental.pallas import tpu as pltpu

F32 = jnp.float32
BF16 = jnp.bfloat16
I32 = jnp.int32

EPS = 1e-6
N_MOD = 6
GRID_W = 64
WIN_ROWS = 8
WIN_COLS = 16
GMLP_CHUNK = 128
GMLP_GROUPS = 8
N_EXPERTS = 16
N_EXPERT_GROUPS = 4
EXPERTS_PER_GROUP = N_EXPERTS // N_EXPERT_GROUPS
TOP_K = 2
MOE_BLOCK = 256
COND_ROWS = 8
ROW_CHUNK = 64
CAST_CHUNK = 256
NA_QROWS = 4
NA_KROWS = 12
NA_KBLOCK = 256
NA_ONES_ROWS = 16
NA_HEADS_PER_STEP = 4
NA_BLOCKS_PER_STEP = 1
LOG2E = 1.4426950408889634
MIB = 1024 * 1024

_NT = (((1,), (1,)), ((), ()))


def _tile(n, pref):
    return pref if n % pref == 0 else n


def _params(n_axes, vmem_mib):
    return pltpu.CompilerParams(dimension_semantics=("arbitrary",) * n_axes,
                                vmem_limit_bytes=vmem_mib * MIB)


def _for_chunks(n, c, fn):
    c = min(c, n)

    def body(t, carry):
        fn(pl.ds(pl.multiple_of(t * c, c), c))
        return carry

    lax.fori_loop(0, n // c, body, 0)


def _sigmoid(x):
    return 1.0 / (1.0 + jnp.exp(-x))


def _gelu_tanh(x):
    return x * (0.5 * (1.0 + jnp.tanh(0.7978845608028654 * (x + 0.044715 * (x * x * x)))))


def _rms_scale(x):
    return x * lax.rsqrt(jnp.mean(x * x, axis=-1, keepdims=True) + EPS)


def _ada_kernel(c_ref, w_ref, b_ref, o_ref):
    c = c_ref[...]
    s = (c * _sigmoid(c)).astype(BF16)
    o_ref[0] = jnp.dot(s, w_ref[0].astype(BF16), preferred_element_type=F32) + b_ref[0]


def _modulation_all(cond, w_ada, b_ada):
    L, D, N = w_ada.shape
    tn = _tile(N, 1024)
    out = pl.pallas_call(
        _ada_kernel,
        grid=(L, N // tn),
        in_specs=[pl.BlockSpec((COND_ROWS, D), lambda l, j: (0, 0)),
                  pl.BlockSpec((1, D, tn), lambda l, j: (l, 0, j)),
                  pl.BlockSpec((1, 1, tn), lambda l, j: (l, 0, j))],
        out_specs=pl.BlockSpec((1, COND_ROWS, tn), lambda l, j: (l, 0, j)),
        out_shape=jax.ShapeDtypeStruct((L, COND_ROWS, N), F32),
        compiler_params=_params(2, 40),
        name="ada_modulation",
    )(cond, w_ada, b_ada.reshape(L, 1, N))
    return out.reshape(L, COND_ROWS, N_MOD, D)


def _prologue_kernel(xp_ref, xs_ref, mod_ref, g_ref, x_ref, h_ref, *, ncb):
    i = pl.program_id(0)
    g = g_ref[0]
    scale1 = 1.0 + mod_ref[0, 0, 1:2, :]
    shift1 = mod_ref[0, 0, 0:1, :]

    def run(src_ref):
        def chunk(rows):
            x = src_ref[rows, :]
            x_ref[rows, :] = x
            h_ref[rows, :] = (_rms_scale(x) * g * scale1 + shift1).astype(BF16)

        _for_chunks(x_ref.shape[0], ROW_CHUNK, chunk)

    pl.when(i < ncb)(lambda: run(xp_ref))
    pl.when(i >= ncb)(lambda: run(xs_ref))


def _prologue(xp, xs, mod, norm_g, cond_of_block, tm):
    nc, D = xp.shape
    T = nc + xs.shape[0]
    ncb = nc // tm
    row = pl.BlockSpec((tm, D), lambda i: (i, 0))
    return pl.pallas_call(
        functools.partial(_prologue_kernel, ncb=ncb),
        grid=(T // tm,),
        in_specs=[pl.BlockSpec((tm, D), lambda i: (jnp.minimum(i, ncb - 1), 0)),
                  pl.BlockSpec((tm, D), lambda i: (jnp.maximum(i - ncb, 0), 0)),
                  pl.BlockSpec((1, 1, N_MOD, D), lambda i: (0, cond_of_block(i), 0, 0)),
                  pl.BlockSpec((1, 1, D), lambda i: (0, 0, 0))],
        out_specs=[row, row],
        out_shape=[jax.ShapeDtypeStruct((T, D), F32), jax.ShapeDtypeStruct((T, D), BF16)],
        compiler_params=_params(1, 40),
        name="concat_norm1",
    )(xp, xs, mod, norm_g.reshape(-1, 1, D))


def _proj_kernel(h_ref, w_ref, nk_in, nv_in, o_ref, k_ref, v_ref, w_scr, *, ncb, jk0, nk, qscale):
    del nk_in, nv_in
    j = pl.program_id(0)
    i = pl.program_id(1)

    @pl.when(i == 0)
    def _():
        def chunk(rows):
            w_scr[rows, :] = w_ref[0, rows, :].astype(BF16)

        _for_chunks(w_scr.shape[0], CAST_CHUNK, chunk)

    acc = jnp.dot(h_ref[...], w_scr[...], preferred_element_type=F32)
    o_ref[...] = (acc * jnp.where(j < jk0, qscale, 1.0)).astype(BF16)

    def put(dst_ref):
        seq = dst_ref.shape[2]
        for b in range(dst_ref.shape[0]):
            dst_ref[b, 0] = acc[b * seq:(b + 1) * seq, :]

    pl.when((i < ncb) & (j >= jk0) & (j < jk0 + nk))(lambda: put(k_ref))
    pl.when((i < ncb) & (j >= jk0 + nk) & (j < jk0 + 2 * nk))(lambda: put(v_ref))


def _input_projection(h, w_in, new_k, new_v, l, tm, qscale):
    T, D = h.shape
    N = w_in.shape[-1]
    batch, _, seq, _ = new_k.shape
    tn = _tile(D, 1024)
    ncb = batch * seq // tm
    jk0, nk = D // tn, D // tn
    kernel = functools.partial(_proj_kernel, ncb=ncb, jk0=jk0, nk=nk, qscale=qscale)

    def cache_block(j0):
        def index(j, i):
            inside = (j >= j0) & (j < j0 + nk)
            row = jnp.where(j < j0, 0, jnp.where(inside, jnp.minimum(i, ncb - 1), ncb - 1))
            col = jnp.where(j < j0, 0, jnp.where(inside, j - j0, nk - 1))
            return (row, l, 0, col)
        return pl.BlockSpec((tm // seq, 1, seq, tn), index)

    return pl.pallas_call(
        kernel,
        grid=(N // tn, T // tm),
        in_specs=[pl.BlockSpec((tm, D), lambda j, i: (i, 0)),
                  pl.BlockSpec((1, D, tn), lambda j, i: (l, 0, j)),
                  pl.BlockSpec(memory_space=pl.ANY),
                  pl.BlockSpec(memory_space=pl.ANY)],
        out_specs=[pl.BlockSpec((tm, tn), lambda j, i: (i, j)), cache_block(jk0), cache_block(jk0 + nk)],
        out_shape=[jax.ShapeDtypeStruct((T, N), BF16),
                   jax.ShapeDtypeStruct(new_k.shape, F32),
                   jax.ShapeDtypeStruct(new_v.shape, F32)],
        scratch_shapes=[pltpu.VMEM((D, tn), BF16)],
        input_output_aliases={2: 1, 3: 2},
        compiler_params=_params(2, 60),
        name="input_projection",
    )(h, w_in, new_k, new_v)


def _gmlp_kernel(u_ref, gv_ref, g_ref, ws_ref, bs_ref, o_ref):
    gw = u_ref.shape[1] // GMLP_GROUPS
    g = g_ref[0]

    def chunk(rows):
        v = _rms_scale(_gelu_tanh(gv_ref[rows, :].astype(F32))) * g
        v = v.astype(BF16)
        for grp in range(GMLP_GROUPS):
            cols = slice(grp * gw, (grp + 1) * gw)
            s = jnp.dot(ws_ref[grp], v[:, cols], preferred_element_type=F32) + bs_ref[grp]
            u = _gelu_tanh(u_ref[rows, cols].astype(F32))
            o_ref[rows, cols] = (u * s).astype(BF16)

    _for_chunks(u_ref.shape[0], GMLP_CHUNK, chunk)


def _gmlp(proj, gmlp_g, w_s, b_s, l, D, tm):
    T = proj.shape[0]
    gw = D // GMLP_GROUPS
    bs = jnp.broadcast_to(b_s[l][:, :, None], (GMLP_GROUPS, GMLP_CHUNK, gw))
    return pl.pallas_call(
        _gmlp_kernel,
        grid=(T // tm,),
        in_specs=[pl.BlockSpec((tm, D), lambda i: (i, 3)),
                  pl.BlockSpec((tm, D), lambda i: (i, 4)),
                  pl.BlockSpec((1, 1, D), lambda i: (l, 0, 0)),
                  pl.BlockSpec((None, GMLP_GROUPS, GMLP_CHUNK, GMLP_CHUNK), lambda i: (l, 0, 0, 0)),
                  pl.BlockSpec((GMLP_GROUPS, GMLP_CHUNK, gw), lambda i: (0, 0, 0))],
        out_specs=pl.BlockSpec((tm, D), lambda i: (i, 0)),
        out_shape=jax.ShapeDtypeStruct((T, D), BF16),
        compiler_params=_params(1, 40),
        name="gmlp_spatial_gating",
    )(proj, proj, gmlp_g.reshape(-1, 1, D), w_s, bs)


def _ctx_attn_kernel(q_ref, k_ref, v_ref, o_ref, *, heads, dh):
    for h in range(heads):
        cols = slice(h * dh, (h + 1) * dh)
        s = lax.dot_general(q_ref[:, cols], k_ref[:, cols], _NT, preferred_element_type=F32)
        p = jnp.exp2(s - jnp.max(s, axis=-1, keepdims=True))
        denom = jnp.sum(p, axis=-1, keepdims=True)
        o = jnp.dot(p.astype(BF16), v_ref[:, cols], preferred_element_type=F32)
        o_ref[:, cols] = (o / denom).astype(BF16)


def _context_attention(proj, batch, seq, D, heads, dh):
    return pl.pallas_call(
        functools.partial(_ctx_attn_kernel, heads=heads, dh=dh),
        grid=(batch,),
        in_specs=[pl.BlockSpec((seq, D), lambda b: (b, 0)),
                  pl.BlockSpec((seq, D), lambda b: (b, 1)),
                  pl.BlockSpec((seq, D), lambda b: (b, 2))],
        out_specs=pl.BlockSpec((seq, D), lambda b: (b, 0)),
        out_shape=jax.ShapeDtypeStruct((batch * seq, D), BF16),
        compiler_params=_params(1, 32),
        name="context_attention",
    )(proj, proj, proj)


def _na_kernel(q_ref, k_ref, v_ref, kc_ref, vc_ref, *rest, hps, dh, rows):
    um_refs, (o_ref, vt_scr, vct_scr) = rest[:NA_BLOCKS_PER_STEP], rest[NA_BLOCKS_PER_STEP:]
    m = pl.program_id(2)
    win = NA_KROWS * GRID_W
    nkb = win // NA_KBLOCK
    hr = dh + NA_ONES_ROWS

    @pl.when(m == 0)
    def _():
        def fill(dst, vt):
            for h in range(hps):
                dst[h * hr:h * hr + dh, :] = vt[h * dh:(h + 1) * dh, :]
                dst[h * hr + dh:(h + 1) * hr, :] = jnp.ones((NA_ONES_ROWS, vt.shape[1]), BF16)

        for kb in range(v_ref.shape[0] // NA_KBLOCK):
            fill(vt_scr.at[kb], v_ref[kb * NA_KBLOCK:(kb + 1) * NA_KBLOCK, :].astype(F32).T.astype(BF16))
        fill(vct_scr, vc_ref[0].astype(F32).T.astype(BF16))

    qn = NA_QROWS * GRID_W
    units = [(sub, h) for sub in range(len(um_refs)) for h in range(hps)]
    kb0 = []
    scores = []
    for sub, um_ref in enumerate(um_refs):
        ws = jnp.clip((m * len(um_refs) + sub) * NA_QROWS - WIN_ROWS // 2, 0, rows - NA_KROWS)
        krows = pl.ds(pl.multiple_of(ws * GRID_W, NA_KBLOCK), win)
        kb0.append(ws // (NA_KBLOCK // GRID_W))
        for h in range(hps):
            cols = slice(h * dh, (h + 1) * dh)
            q = q_ref[sub * qn:(sub + 1) * qn, cols]
            s_loc = lax.dot_general(k_ref[krows, cols], q, _NT, preferred_element_type=F32) + um_ref[0, 0, h]
            s_ctx = lax.dot_general(kc_ref[0, :, cols], q, _NT, preferred_element_type=F32)
            scores.append((s_loc, s_ctx))
    probs = []
    for s_loc, s_ctx in scores:
        mx = jnp.maximum(jnp.max(s_loc, axis=0, keepdims=True), jnp.max(s_ctx, axis=0, keepdims=True))
        probs.append((jnp.exp2(s_loc - mx).astype(BF16), jnp.exp2(s_ctx - mx).astype(BF16)))
    for (sub, h), (p_loc, p_ctx) in zip(units, probs):
        hrows = slice(h * hr, (h + 1) * hr)
        ot = jnp.dot(vct_scr[hrows, :], p_ctx, preferred_element_type=F32)
        for t in range(nkb):
            ot = ot + jnp.dot(vt_scr[kb0[sub] + t, hrows, :], p_loc[t * NA_KBLOCK:(t + 1) * NA_KBLOCK, :],
                              preferred_element_type=F32)
        o_ref[sub * qn:(sub + 1) * qn, h * dh:(h + 1) * dh] = (ot[:dh] / ot[dh:dh + 1]).T.astype(BF16)


def _na_tables(rpb, rows):
    L, H = rpb.shape[:2]
    kh = WIN_ROWS
    col = np.arange(GRID_W)
    cs = np.clip(col - WIN_COLS // 2, 0, GRID_W - WIN_COLS)
    col_valid = (col[:, None] >= cs[None, :]) & (col[:, None] < cs[None, :] + WIN_COLS)
    n_rho = NA_KROWS + WIN_ROWS
    n_dr = 2 * WIN_ROWS - 1
    j = np.arange(NA_QROWS)
    flipped = jnp.pad(jnp.flip(rpb.astype(F32), axis=-1), ((0, 0), (0, 0), (0, 0), (GRID_W, GRID_W)))
    t1 = jnp.stack([flipped[..., GRID_W + WIN_COLS - 1 - kc:2 * GRID_W + WIN_COLS - 1 - kc] for kc in range(GRID_W)],
                   axis=3)
    t1 = jnp.where(col_valid, t1, -jnp.inf)
    lanes = []
    for jj in range(NA_QROWS):
        dr = np.clip(np.arange(n_rho) - WIN_ROWS - jj + WIN_ROWS - 1, 0, n_dr - 1)
        lo, hi = int((dr == 0).sum()) - 1, int((dr == n_dr - 1).sum()) - 1
        assert lo + n_dr + hi == n_rho
        lanes.append(jnp.concatenate([t1[:, :, :1]] * lo + [t1] + [t1[:, :, -1:]] * hi, axis=2))
    u = jnp.concatenate(lanes, axis=-1).reshape(L, H, n_rho * GRID_W, NA_QROWS * GRID_W)

    def band(r0, ws):
        rs = np.clip(r0 + j - kh // 2, 0, rows - kh)
        key_row = ws + np.arange(NA_KROWS)
        ok = (key_row[:, None] >= rs[None, :]) & (key_row[:, None] < rs[None, :] + kh)
        mk = np.where(ok, 0.0, -np.inf).astype(np.float32)
        return np.broadcast_to(mk[:, None, :, None], (NA_KROWS, GRID_W, NA_QROWS, GRID_W)).reshape(
            NA_KROWS * GRID_W, NA_QROWS * GRID_W)

    win = NA_KROWS * GRID_W
    kinds = []
    for r0, ws in ((0, 0), (NA_QROWS, 0), (rows - NA_QROWS, rows - NA_KROWS)):
        start = (ws - r0 + WIN_ROWS) * GRID_W
        kinds.append(u[:, :, start:start + win] * LOG2E + jnp.asarray(band(r0, ws)))
    return jnp.stack(kinds, axis=1)


def _neighbourhood_attention(proj, kc, vc, um, l, nc, dec_batch, dec_seq, D, heads, dh):
    rows = dec_seq // GRID_W
    nsub = NA_BLOCKS_PER_STEP
    assert rows % (NA_QROWS * nsub) == 0 and rows >= NA_KROWS
    hps = min(NA_HEADS_PER_STEP, heads)
    qn = nsub * NA_QROWS * GRID_W
    cw = hps * dh
    n_hg = heads // hps
    nblk = rows // NA_QROWS
    nstep = nblk // nsub
    q0 = nc // qn
    k0 = nc // dec_seq
    ncols = D // cw
    past = kc.shape[1]
    hr = dh + NA_ONES_ROWS
    kind = lambda blk: jnp.where(blk == 0, 0, jnp.where(blk == nblk - 1, 2, 1))

    def table_spec(sub):
        return pl.BlockSpec((1, 1, hps) + um.shape[3:], lambda b, g, m: (l, kind(m * nsub + sub), g, 0, 0))

    kernel = functools.partial(_na_kernel, hps=hps, dh=dh, rows=rows)
    return pl.pallas_call(
        kernel,
        grid=(dec_batch, n_hg, nstep),
        in_specs=[pl.BlockSpec((qn, cw), lambda b, g, m: (q0 + b * nstep + m, g)),
                  pl.BlockSpec((dec_seq, cw), lambda b, g, m: (k0 + b, ncols + g)),
                  pl.BlockSpec((dec_seq, cw), lambda b, g, m: (k0 + b, 2 * ncols + g)),
                  pl.BlockSpec((1, past, cw), lambda b, g, m: (b, 0, g)),
                  pl.BlockSpec((1, past, cw), lambda b, g, m: (b, 0, g))] + [table_spec(s) for s in range(nsub)],
        out_specs=pl.BlockSpec((qn, cw), lambda b, g, m: (b * nstep + m, g)),
        out_shape=jax.ShapeDtypeStruct((dec_batch * dec_seq, D), BF16),
        scratch_shapes=[pltpu.VMEM((dec_seq // NA_KBLOCK, hps * hr, NA_KBLOCK), BF16),
                        pltpu.VMEM((hps * hr, past), BF16)],
        compiler_params=_params(3, 56),
        name="neighbourhood_attention",
    )(proj, proj, proj, kc, vc, *([um] * nsub))


def _merge_kernel(ya_ref, ybc_ref, ybl_ref, ga_ref, gb_ref, wa_ref, wb_ref, o_ref, *, ncb):
    yb = jnp.where(pl.program_id(0) < ncb, ybc_ref[...], ybl_ref[...])
    a = jnp.dot(ya_ref[...], wa_ref[0], preferred_element_type=F32)
    b = jnp.dot(yb, wb_ref[0], preferred_element_type=F32)
    o = _sigmoid(ga_ref[...].astype(F32)) * a + _sigmoid(gb_ref[...].astype(F32)) * b
    o_ref[...] = o.astype(BF16)


def _merge(ya, yb_ctx, yb_lat, proj, w_pa, w_pb, l, D, tm):
    T = ya.shape[0]
    tn = _tile(D, 512)
    nj = D // tn
    ncb = yb_ctx.shape[0] // tm
    return pl.pallas_call(
        functools.partial(_merge_kernel, ncb=ncb),
        grid=(T // tm, nj),
        in_specs=[pl.BlockSpec((tm, D), lambda i, j: (i, 0)),
                  pl.BlockSpec((tm, D), lambda i, j: (jnp.minimum(i, ncb - 1), 0)),
                  pl.BlockSpec((tm, D), lambda i, j: (jnp.maximum(i - ncb, 0), 0)),
                  pl.BlockSpec((tm, tn), lambda i, j: (i, 5 * nj + j)),
                  pl.BlockSpec((tm, tn), lambda i, j: (i, 6 * nj + j)),
                  pl.BlockSpec((1, D, tn), lambda i, j: (l, 0, j)),
                  pl.BlockSpec((1, D, tn), lambda i, j: (l, 0, j))],
        out_specs=pl.BlockSpec((tm, tn), lambda i, j: (i, j)),
        out_shape=jax.ShapeDtypeStruct((T, D), BF16),
        compiler_params=_params(2, 56),
        name="gated_branch_merge",
    )(ya, yb_ctx, yb_lat, proj, proj, w_pa, w_pb)


def _argmax_first(vals):
    best = vals[0]
    idx = jnp.zeros(best.shape, I32)
    for k in range(1, len(vals)):
        upd = vals[k] > best
        idx = jnp.where(upd, k, idx)
        best = jnp.where(upd, vals[k], best)
    return idx


def _select(idx, vals):
    out = vals[-1]
    for k in range(len(vals) - 2, -1, -1):
        out = jnp.where(idx == k, vals[k], out)
    return out


def _route(logits_t, router_b):
    m = jnp.max(logits_t, axis=0, keepdims=True)
    ex = jnp.exp(logits_t - m)
    probs = ex / jnp.sum(ex, axis=0, keepdims=True)
    sel = probs + router_b
    sel_rows = [sel[e:e + 1, :] for e in range(N_EXPERTS)]
    prob_rows = [probs[e:e + 1, :] for e in range(N_EXPERTS)]
    scores = []
    for g in range(N_EXPERT_GROUPS):
        a, b, c, d = sel_rows[g * EXPERTS_PER_GROUP:(g + 1) * EXPERTS_PER_GROUP]
        hi1, lo1 = jnp.maximum(a, b), jnp.minimum(a, b)
        hi2, lo2 = jnp.maximum(c, d), jnp.minimum(c, d)
        top1 = jnp.maximum(hi1, hi2)
        top2 = jnp.maximum(jnp.minimum(hi1, hi2), jnp.where(hi1 >= hi2, lo1, lo2))
        scores.append(top1 + top2)
    best_g = _argmax_first(scores)
    in_sel = [_select(best_g, [sel_rows[g * EXPERTS_PER_GROUP + k] for g in range(N_EXPERT_GROUPS)])
              for k in range(EXPERTS_PER_GROUP)]
    in_prob = [_select(best_g, [prob_rows[g * EXPERTS_PER_GROUP + k] for g in range(N_EXPERT_GROUPS)])
               for k in range(EXPERTS_PER_GROUP)]
    i1 = _argmax_first(in_sel)
    i2 = _argmax_first([jnp.where(i1 == k, -jnp.inf, in_sel[k]) for k in range(EXPERTS_PER_GROUP)])
    w1 = _select(i1, in_prob)
    w2 = _select(i2, in_prob)
    tot = w1 + w2
    base = best_g * EXPERTS_PER_GROUP
    return (base + i1, base + i2), (w1 / tot, w2 / tot)


def _out_kernel(mg_ref, x_ref, mod_ref, g_ref, w_ref, rw_ref, rb_ref, xo_ref, h_ref, e_ref, gt_ref, acc0, acc1):
    s = pl.program_id(0)

    @pl.when(s == 0)
    def _():
        acc1[...] = jnp.zeros(acc1.shape, acc1.dtype)

    def step(acc_w, acc_r):
        gate1 = mod_ref[0, 0, 2:3, :]
        shift2 = mod_ref[0, 0, 3:4, :]
        scale2 = 1.0 + mod_ref[0, 0, 4:5, :]
        g = g_ref[0]
        acc_w[...] = jnp.dot(mg_ref[...], w_ref[0], preferred_element_type=F32)
        for c in range(x_ref.shape[0] // ROW_CHUNK):
            rows = slice(c * ROW_CHUNK, (c + 1) * ROW_CHUNK)
            xn = x_ref[rows, :] + gate1 * acc_r[rows, :]
            xo_ref[rows, :] = xn
            h_ref[rows, :] = (_rms_scale(xn) * g * scale2 + shift2).astype(BF16)
        logits_t = lax.dot_general(rw_ref[...], h_ref[...], _NT, preferred_element_type=F32)
        experts, gates = _route(logits_t, rb_ref[...])
        for k in range(TOP_K):
            e_ref[k:k + 1, :] = experts[k]
            gt_ref[k:k + 1, :] = gates[k]

    pl.when(s % 2 == 0)(lambda: step(acc0, acc1))
    pl.when(s % 2 == 1)(lambda: step(acc1, acc0))


def _output_projection(merged, x, mod, norm_g, w_out, router_wt, router_b, l, cond_of_block, tm):
    T, D = x.shape
    E = router_wt.shape[0]
    n = T // tm
    cur = lambda s: jnp.minimum(s, n - 1)
    prev = lambda s: jnp.maximum(s - 1, 0)
    return pl.pallas_call(
        _out_kernel,
        grid=(n + 1,),
        in_specs=[pl.BlockSpec((tm, D), lambda s: (cur(s), 0)),
                  pl.BlockSpec((tm, D), lambda s: (prev(s), 0)),
                  pl.BlockSpec((1, 1, N_MOD, D), lambda s: (l, cond_of_block(prev(s)), 0, 0)),
                  pl.BlockSpec((1, 1, D), lambda s: (l, 0, 0)),
                  pl.BlockSpec((1, D, D), lambda s: (l, 0, 0)),
                  pl.BlockSpec((E, D), lambda s: (0, 0)),
                  pl.BlockSpec((E, 1), lambda s: (0, 0))],
        out_specs=[pl.BlockSpec((tm, D), lambda s: (prev(s), 0)),
                   pl.BlockSpec((tm, D), lambda s: (prev(s), 0)),
                   pl.BlockSpec((TOP_K, tm), lambda s: (0, prev(s))),
                   pl.BlockSpec((TOP_K, tm), lambda s: (0, prev(s)))],
        out_shape=[jax.ShapeDtypeStruct((T, D), F32),
                   jax.ShapeDtypeStruct((T, D), BF16),
                   jax.ShapeDtypeStruct((TOP_K, T), I32),
                   jax.ShapeDtypeStruct((TOP_K, T), F32)],
        scratch_shapes=[pltpu.VMEM((tm, D), F32), pltpu.VMEM((tm, D), F32)],
        compiler_params=_params(1, 56),
        name="output_projection_norm2_router",
    )(merged, x, mod, norm_g.reshape(-1, 1, D), w_out, router_wt, router_b.reshape(E, 1))


def _moe_kernel(blk_ref, exp_ref, lo_ref, hi_ref, x_ref, wg_ref, wu_ref, wd_ref, o_ref, wg_b, wu_b, wd_b):
    w = pl.program_id(0)
    lo, hi = lo_ref[w], hi_ref[w]
    row0 = blk_ref[w] * MOE_BLOCK

    @pl.when((w == 0) | (exp_ref[w] != exp_ref[jnp.maximum(w - 1, 0)]))
    def _():
        for src, dst in ((wg_ref, wg_b), (wu_ref, wu_b), (wd_ref, wd_b)):
            def chunk(rows, src=src, dst=dst):
                dst[rows, :] = src[0, rows, :].astype(BF16)

            _for_chunks(dst.shape[0], CAST_CHUNK, chunk)

    @pl.when(lo == row0)
    def _():
        o_ref[...] = jnp.zeros(o_ref.shape, o_ref.dtype)

    @pl.when(hi > lo)
    def _():
        x = x_ref[...]
        g = jnp.dot(x, wg_b[...], preferred_element_type=F32)
        u = jnp.dot(x, wu_b[...], preferred_element_type=F32)
        hid = (g * _sigmoid(g)) * u
        y = jnp.dot(hid.astype(BF16), wd_b[...], preferred_element_type=F32)
        row = row0 + lax.broadcasted_iota(I32, (MOE_BLOCK, 1), 0)
        o_ref[...] += jnp.where((row >= lo) & (row < hi), y, 0.0).astype(o_ref.dtype)


def _moe_experts(xs, items, w_gate, w_up, w_down, l):
    A, D = xs.shape
    E, DE = w_gate.shape[1], w_gate.shape[3]
    w_gate = w_gate.reshape(-1, D, DE)
    w_up = w_up.reshape(-1, D, DE)
    w_down = w_down.reshape(-1, DE, D)
    n_items = items[0].shape[0]
    expert = lambda w, blk, ex, lo, hi: (l * E + ex[w], 0, 0)
    grid_spec = pltpu.PrefetchScalarGridSpec(
        num_scalar_prefetch=4,
        grid=(n_items,),
        in_specs=[pl.BlockSpec((MOE_BLOCK, D), lambda w, blk, ex, lo, hi: (blk[w], 0)),
                  pl.BlockSpec((1, D, DE), expert, pipeline_mode=pl.Buffered(1)),
                  pl.BlockSpec((1, D, DE), expert, pipeline_mode=pl.Buffered(1)),
                  pl.BlockSpec((1, DE, D), expert)],
        out_specs=pl.BlockSpec((MOE_BLOCK, D), lambda w, blk, ex, lo, hi: (blk[w], 0)),
        scratch_shapes=[pltpu.VMEM((D, DE), BF16), pltpu.VMEM((D, DE), BF16), pltpu.VMEM((DE, D), BF16)],
    )
    return pl.pallas_call(
        _moe_kernel,
        grid_spec=grid_spec,
        out_shape=jax.ShapeDtypeStruct((A, D), BF16),
        compiler_params=_params(1, 58),
        name="moe_experts",
    )(*items, xs, w_gate, w_up, w_down)


def _dispatch_plan(experts, T):
    A = TOP_K * T
    nb = A // MOE_BLOCK
    n_items = nb + N_EXPERTS - 1
    e_flat = experts.reshape(A)
    iota = jnp.arange(A, dtype=I32)
    _, order = lax.sort((e_flat, iota), num_keys=1, is_stable=True)
    _, rank = lax.sort((order, iota), num_keys=1)
    eid = jnp.arange(N_EXPERTS, dtype=I32)
    counts = jnp.sum((e_flat[:, None] == eid[None, :]).astype(I32), axis=0)
    ends = jnp.cumsum(counts)
    starts = ends - counts
    first_blk = starts // MOE_BLOCK
    n_blk = jnp.where(counts > 0, (ends - 1) // MOE_BLOCK - first_blk + 1, 0)
    item_end = jnp.cumsum(n_blk)
    item_start = item_end - n_blk
    w = jnp.arange(n_items, dtype=I32)
    valid = w < item_end[-1]
    ex = jnp.minimum(jnp.sum((w[:, None] >= item_end[None, :]).astype(I32), axis=1), N_EXPERTS - 1)
    pick = lambda tbl: jnp.sum(jnp.where(ex[:, None] == eid[None, :], tbl[None, :], 0), axis=1)
    blk = jnp.where(valid, pick(first_blk) + w - pick(item_start), nb - 1)
    lo = jnp.where(valid, jnp.maximum(pick(starts), blk * MOE_BLOCK), 0)
    hi = jnp.where(valid, jnp.minimum(pick(ends), (blk + 1) * MOE_BLOCK), 0)
    ex = jnp.where(valid, ex, jnp.max(jnp.where(counts > 0, eid, 0)))
    return order % T, rank, (blk.astype(I32), ex.astype(I32), lo.astype(I32), hi.astype(I32))


def _combined(x_ref, y1_ref, y2_ref, g1_ref, g2_ref, gate2, rows):
    y = g1_ref[rows, :] * y1_ref[rows, :].astype(F32) + g2_ref[rows, :] * y2_ref[rows, :].astype(F32)
    return x_ref[rows, :] + gate2 * y


def _combine_kernel(x_ref, y1_ref, y2_ref, g1_ref, g2_ref, mod_ref, modn_ref, gn_ref, xo_ref, h_ref):
    gate2 = mod_ref[0, 0, 5:6, :]
    scale1 = 1.0 + modn_ref[0, 0, 1:2, :]
    shift1 = modn_ref[0, 0, 0:1, :]
    g = gn_ref[0]

    def chunk(rows):
        xn = _combined(x_ref, y1_ref, y2_ref, g1_ref, g2_ref, gate2, rows)
        xo_ref[rows, :] = xn
        h_ref[rows, :] = (_rms_scale(xn) * g * scale1 + shift1).astype(BF16)

    _for_chunks(x_ref.shape[0], ROW_CHUNK, chunk)


def _combine_final_kernel(x_ref, y1_ref, y2_ref, g1_ref, g2_ref, mod_ref, gf_ref, yp_ref, ys_ref, *, ncb):
    i = pl.program_id(0)
    gate2 = mod_ref[0, 0, 5:6, :]
    g = gf_ref[...]

    def run(dst_ref):
        def chunk(rows):
            xn = _combined(x_ref, y1_ref, y2_ref, g1_ref, g2_ref, gate2, rows)
            dst_ref[rows, :] = _rms_scale(xn) * g

        _for_chunks(x_ref.shape[0], ROW_CHUNK, chunk)

    pl.when(i < ncb)(lambda: run(yp_ref))
    pl.when(i >= ncb)(lambda: run(ys_ref))


def _combine(x, y1, y2, g1, g2, mod, norm1_g, l, cond_of_block, tm):
    T, D = x.shape
    row = pl.BlockSpec((tm, D), lambda i: (i, 0))
    col = pl.BlockSpec((tm, 1), lambda i: (i, 0))
    return pl.pallas_call(
        _combine_kernel,
        grid=(T // tm,),
        in_specs=[row, row, row, col, col,
                  pl.BlockSpec((1, 1, N_MOD, D), lambda i: (l, cond_of_block(i), 0, 0)),
                  pl.BlockSpec((1, 1, N_MOD, D), lambda i: (l + 1, cond_of_block(i), 0, 0)),
                  pl.BlockSpec((1, 1, D), lambda i: (l + 1, 0, 0))],
        out_specs=[row, row],
        out_shape=[jax.ShapeDtypeStruct((T, D), F32), jax.ShapeDtypeStruct((T, D), BF16)],
        compiler_params=_params(1, 52),
        name="moe_combine_residual_norm1",
    )(x, y1, y2, g1, g2, mod, mod, norm1_g.reshape(-1, 1, D))


def _combine_final(x, y1, y2, g1, g2, mod, final_g, l, nc, cond_of_block, tm):
    T, D = x.shape
    ncb = nc // tm
    row = pl.BlockSpec((tm, D), lambda i: (i, 0))
    col = pl.BlockSpec((tm, 1), lambda i: (i, 0))
    return pl.pallas_call(
        functools.partial(_combine_final_kernel, ncb=ncb),
        grid=(T // tm,),
        in_specs=[row, row, row, col, col,
                  pl.BlockSpec((1, 1, N_MOD, D), lambda i: (l, cond_of_block(i), 0, 0)),
                  pl.BlockSpec((1, D), lambda i: (0, 0))],
        out_specs=[pl.BlockSpec((tm, D), lambda i: (jnp.minimum(i, ncb - 1), 0)),
                   pl.BlockSpec((tm, D), lambda i: (jnp.maximum(i - ncb, 0), 0))],
        out_shape=[jax.ShapeDtypeStruct((nc, D), F32), jax.ShapeDtypeStruct((T - nc, D), F32)],
        compiler_params=_params(1, 52),
        name="moe_combine_residual_final_norm",
    )(x, y1, y2, g1, g2, mod, final_g.reshape(1, D))


def kernel(x_prompt, x_sample, cache_k, cache_v, c, c_ctx, w_ada, b_ada, norm1_g, w_in, gmlp_g, w_s, b_s, rpb,
           w_proj_a, w_proj_b, w_out, norm2_g, router_w, router_b, w_gate, w_up, w_down, final_g):
    batch, seq, D = x_prompt.shape
    dec_batch, dec_seq, _ = x_sample.shape
    depth = w_in.shape[0]
    heads, dh = cache_k.shape[3], cache_k.shape[4]
    past = cache_k.shape[2]
    nc, nl = batch * seq, dec_batch * dec_seq
    T = nc + nl
    assert heads * dh == D and gmlp_g.shape[1] == D and w_in.shape[2] == 7 * D
    assert 1 + dec_batch <= COND_ROWS
    assert router_w.shape[1] == N_EXPERTS
    assert seq % GMLP_CHUNK == 0 and dec_seq % GMLP_CHUNK == 0 and dec_seq % GRID_W == 0
    assert nc % dec_seq == 0 and (TOP_K * T) % MOE_BLOCK == 0

    tm_big = 1024 if (nc % 1024 == 0 and dec_seq % 1024 == 0) else GMLP_CHUNK
    tm_mid = 512 if (nc % 512 == 0 and dec_seq % 512 == 0) else GMLP_CHUNK

    def cond_of(tm):
        ncb, per = nc // tm, dec_seq // tm
        return lambda i: jnp.where(i < ncb, 0, 1 + (i - ncb) // per)

    cond = jnp.zeros((COND_ROWS, D), F32).at[0].set(c_ctx).at[1:1 + dec_batch].set(c)
    mod = _modulation_all(cond, w_ada, b_ada)

    w_s_b = w_s.astype(BF16)
    w_pa_b = w_proj_a.astype(BF16)
    w_pb_b = w_proj_b.astype(BF16)
    w_out_b = w_out.astype(BF16)
    kc_all = cache_k.astype(BF16).reshape(dec_batch, depth, past, D)
    vc_all = cache_v.astype(BF16).reshape(dec_batch, depth, past, D)
    router_wt = router_w.T.astype(BF16)
    na_tables = _na_tables(rpb, dec_seq // GRID_W)

    x, h = _prologue(x_prompt.reshape(nc, D), x_sample.reshape(nl, D), mod, norm1_g, cond_of(tm_mid), tm_mid)
    new_k = jnp.zeros((batch, depth, seq, D), F32)
    new_v = jnp.zeros((batch, depth, seq, D), F32)
    for l in range(depth):
        proj, new_k, new_v = _input_projection(h, w_in, new_k, new_v, l, tm_big, dh ** -0.5 * LOG2E)
        ya = _gmlp(proj, gmlp_g, w_s_b, b_s, l, D, tm_mid)
        yb_ctx = _context_attention(proj, batch, seq, D, heads, dh)
        yb_lat = _neighbourhood_attention(proj, kc_all[:, l], vc_all[:, l], na_tables, l, nc, dec_batch,
                                          dec_seq, D, heads, dh)
        merged = _merge(ya, yb_ctx, yb_lat, proj, w_pa_b, w_pb_b, l, D, tm_big)
        x, h2, experts, gates = _output_projection(merged, x, mod, norm2_g, w_out_b, router_wt, router_b, l,
                                                   cond_of(tm_mid), tm_mid)
        tok_sorted, rank, items = _dispatch_plan(experts, T)
        yexp = _moe_experts(h2[tok_sorted], items, w_gate, w_up, w_down, l)
        y1, y2, g1, g2 = yexp[rank[:T]], yexp[rank[T:]], gates[0][:, None], gates[1][:, None]
        if l + 1 < depth:
            x, h = _combine(x, y1, y2, g1, g2, mod, norm1_g, l, cond_of(tm_mid), tm_mid)
        else:
            y_prompt, y_sample = _combine_final(x, y1, y2, g1, g2, mod, final_g, l, nc, cond_of(tm_mid), tm_mid)

    return (y_prompt.reshape(batch, seq, D), y_sample.reshape(dec_batch, dec_seq, D),
            new_k.reshape(batch, depth, seq, heads, dh), new_v.reshape(batch, depth, seq, heads, dh))
```

```python
import functools

import numpy as np
import jax
import jax.numpy as jnp
from jax import lax
from jax.experimental import pallas as pl
from jax.experimental.pallas import tpu as pltpu

F32 = jnp.float32
BF16 = jnp.bfloat16
I32 = jnp.int32

EPS = 1e-6
N_MOD = 6
GRID_W = 64
WIN_ROWS = 8
WIN_COLS = 16
GMLP_CHUNK = 128
GMLP_GROUPS = 8
N_EXPERTS = 16
N_EXPERT_GROUPS = 4
EXPERTS_PER_GROUP = N_EXPERTS // N_EXPERT_GROUPS
TOP_K = 2
MOE_BLOCK = 256
COND_ROWS = 8
ROW_CHUNK = 64
CAST_CHUNK = 256
NA_QROWS = 4
NA_KROWS = 12
NA_KBLOCK = 256
NA_ONES_ROWS = 16
NA_HEADS_PER_STEP = 4
NA_BLOCKS_PER_STEP = 1
LOG2E = 1.4426950408889634
MIB = 1024 * 1024

_NT = (((1,), (1,)), ((), ()))


def _tile(n, pref):
    return pref if n % pref == 0 else n


def _params(n_axes, vmem_mib):
    return pltpu.CompilerParams(dimension_semantics=("arbitrary",) * n_axes,
                                vmem_limit_bytes=vmem_mib * MIB)


def _for_chunks(n, c, fn):
    c = min(c, n)

    def body(t, carry):
        fn(pl.ds(pl.multiple_of(t * c, c), c))
        return carry

    lax.fori_loop(0, n // c, body, 0)


def _sigmoid(x):
    return 1.0 / (1.0 + jnp.exp(-x))


def _gelu_tanh(x):
    return x * (0.5 * (1.0 + jnp.tanh(0.7978845608028654 * (x + 0.044715 * (x * x * x)))))


def _rms_scale(x):
    return x * lax.rsqrt(jnp.mean(x * x, axis=-1, keepdims=True) + EPS)


def _ada_kernel(c_ref, w_ref, b_ref, o_ref):
    c = c_ref[...]
    s = (c * _sigmoid(c)).astype(BF16)
    o_ref[0] = jnp.dot(s, w_ref[0].astype(BF16), preferred_element_type=F32) + b_ref[0]


def _modulation_all(cond, w_ada, b_ada):
    L, D, N = w_ada.shape
    tn = _tile(N, 1024)
    out = pl.pallas_call(
        _ada_kernel,
        grid=(L, N // tn),
        in_specs=[pl.BlockSpec((COND_ROWS, D), lambda l, j: (0, 0)),
                  pl.BlockSpec((1, D, tn), lambda l, j: (l, 0, j)),
                  pl.BlockSpec((1, 1, tn), lambda l, j: (l, 0, j))],
        out_specs=pl.BlockSpec((1, COND_ROWS, tn), lambda l, j: (l, 0, j)),
        out_shape=jax.ShapeDtypeStruct((L, COND_ROWS, N), F32),
        compiler_params=_params(2, 40),
        name="ada_modulation",
    )(cond, w_ada, b_ada.reshape(L, 1, N))
    return out.reshape(L, COND_ROWS, N_MOD, D)


def _prologue_kernel(xp_ref, xs_ref, mod_ref, g_ref, x_ref, h_ref, *, ncb):
    i = pl.program_id(0)
    g = g_ref[0]
    scale1 = 1.0 + mod_ref[0, 0, 1:2, :]
    shift1 = mod_ref[0, 0, 0:1, :]

    def run(src_ref):
        def chunk(rows):
            x = src_ref[rows, :]
            x_ref[rows, :] = x
            h_ref[rows, :] = (_rms_scale(x) * g * scale1 + shift1).astype(BF16)

        _for_chunks(x_ref.shape[0], ROW_CHUNK, chunk)

    pl.when(i < ncb)(lambda: run(xp_ref))
    pl.when(i >= ncb)(lambda: run(xs_ref))


def _prologue(xp, xs, mod, norm_g, cond_of_block, tm):
    nc, D = xp.shape
    T = nc + xs.shape[0]
    ncb = nc // tm
    row = pl.BlockSpec((tm, D), lambda i: (i, 0))
    return pl.pallas_call(
        functools.partial(_prologue_kernel, ncb=ncb),
        grid=(T // tm,),
        in_specs=[pl.BlockSpec((tm, D), lambda i: (jnp.minimum(i, ncb - 1), 0)),
                  pl.BlockSpec((tm, D), lambda i: (jnp.maximum(i - ncb, 0), 0)),
                  pl.BlockSpec((1, 1, N_MOD, D), lambda i: (0, cond_of_block(i), 0, 0)),
                  pl.BlockSpec((1, 1, D), lambda i: (0, 0, 0))],
        out_specs=[row, row],
        out_shape=[jax.ShapeDtypeStruct((T, D), F32), jax.ShapeDtypeStruct((T, D), BF16)],
        compiler_params=_params(1, 40),
        name="concat_norm1",
    )(xp, xs, mod, norm_g.reshape(-1, 1, D))


def _proj_kernel(h_ref, w_ref, nk_in, nv_in, o_ref, k_ref, v_ref, w_scr, *, ncb, jk0, nk, qscale):
    del nk_in, nv_in
    j = pl.program_id(0)
    i = pl.program_id(1)

    @pl.when(i == 0)
    def _():
        def chunk(rows):
            w_scr[rows, :] = w_ref[0, rows, :].astype(BF16)

        _for_chunks(w_scr.shape[0], CAST_CHUNK, chunk)

    acc = jnp.dot(h_ref[...], w_scr[...], preferred_element_type=F32)
    is_gmlp = (j >= jk0 + 2 * nk) & (j < jk0 + 4 * nk)
    o_ref[...] = jnp.where(is_gmlp, _gelu_tanh(acc), acc * jnp.where(j < jk0, qscale, 1.0)).astype(BF16)

    def put(dst_ref):
        seq = dst_ref.shape[2]
        for b in range(dst_ref.shape[0]):
            dst_ref[b, 0] = acc[b * seq:(b + 1) * seq, :]

    pl.when((i < ncb) & (j >= jk0) & (j < jk0 + nk))(lambda: put(k_ref))
    pl.when((i < ncb) & (j >= jk0 + nk) & (j < jk0 + 2 * nk))(lambda: put(v_ref))


def _input_projection(h, w_in, new_k, new_v, l, tm, qscale):
    T, D = h.shape
    N = w_in.shape[-1]
    batch, _, seq, _ = new_k.shape
    tn = _tile(D, 1024)
    ncb = batch * seq // tm
    jk0, nk = D // tn, D // tn
    kernel = functools.partial(_proj_kernel, ncb=ncb, jk0=jk0, nk=nk, qscale=qscale)

    def cache_block(j0):
        def index(j, i):
            inside = (j >= j0) & (j < j0 + nk)
            row = jnp.where(j < j0, 0, jnp.where(inside, jnp.minimum(i, ncb - 1), ncb - 1))
            col = jnp.where(j < j0, 0, jnp.where(inside, j - j0, nk - 1))
            return (row, l, 0, col)
        return pl.BlockSpec((tm // seq, 1, seq, tn), index)

    return pl.pallas_call(
        kernel,
        grid=(N // tn, T // tm),
        in_specs=[pl.BlockSpec((tm, D), lambda j, i: (i, 0)),
                  pl.BlockSpec((1, D, tn), lambda j, i: (l, 0, j)),
                  pl.BlockSpec(memory_space=pl.ANY),
                  pl.BlockSpec(memory_space=pl.ANY)],
        out_specs=[pl.BlockSpec((tm, tn), lambda j, i: (i, j)), cache_block(jk0), cache_block(jk0 + nk)],
        out_shape=[jax.ShapeDtypeStruct((T, N), BF16),
                   jax.ShapeDtypeStruct(new_k.shape, F32),
                   jax.ShapeDtypeStruct(new_v.shape, F32)],
        scratch_shapes=[pltpu.VMEM((D, tn), BF16)],
        input_output_aliases={2: 1, 3: 2},
        compiler_params=_params(2, 60),
        name="input_projection",
    )(h, w_in, new_k, new_v)


def _gmlp_kernel(u_ref, gv_ref, g_ref, ws_ref, bs_ref, o_ref):
    gw = u_ref.shape[1] // GMLP_GROUPS
    g = g_ref[0]

    def chunk(rows):
        v = (_rms_scale(gv_ref[rows, :].astype(F32)) * g).astype(BF16)
        for grp in range(GMLP_GROUPS):
            cols = slice(grp * gw, (grp + 1) * gw)
            s = jnp.dot(ws_ref[grp], v[:, cols], preferred_element_type=F32) + bs_ref[grp]
            o_ref[rows, cols] = (u_ref[rows, cols].astype(F32) * s).astype(BF16)

    _for_chunks(u_ref.shape[0], GMLP_CHUNK, chunk)


def _gmlp(proj, gmlp_g, w_s, b_s, l, D, tm):
    T = proj.shape[0]
    gw = D // GMLP_GROUPS
    bs = jnp.broadcast_to(b_s[l][:, :, None], (GMLP_GROUPS, GMLP_CHUNK, gw))
    return pl.pallas_call(
        _gmlp_kernel,
        grid=(T // tm,),
        in_specs=[pl.BlockSpec((tm, D), lambda i: (i, 3)),
                  pl.BlockSpec((tm, D), lambda i: (i, 4)),
                  pl.BlockSpec((1, 1, D), lambda i: (l, 0, 0)),
                  pl.BlockSpec((None, GMLP_GROUPS, GMLP_CHUNK, GMLP_CHUNK), lambda i: (l, 0, 0, 0)),
                  pl.BlockSpec((GMLP_GROUPS, GMLP_CHUNK, gw), lambda i: (0, 0, 0))],
        out_specs=pl.BlockSpec((tm, D), lambda i: (i, 0)),
        out_shape=jax.ShapeDtypeStruct((T, D), BF16),
        compiler_params=_params(1, 40),
        name="gmlp_spatial_gating",
    )(proj, proj, gmlp_g.reshape(-1, 1, D), w_s, bs)


def _ctx_attn_kernel(q_ref, k_ref, v_ref, o_ref, *, heads, dh):
    for h in range(heads):
        cols = slice(h * dh, (h + 1) * dh)
        s = lax.dot_general(q_ref[:, cols], k_ref[:, cols], _NT, preferred_element_type=F32)
        p = jnp.exp2(s - jnp.max(s, axis=-1, keepdims=True))
        denom = jnp.sum(p, axis=-1, keepdims=True)
        o = jnp.dot(p.astype(BF16), v_ref[:, cols], preferred_element_type=F32)
        o_ref[:, cols] = (o / denom).astype(BF16)


def _context_attention(proj, batch, seq, D, heads, dh):
    return pl.pallas_call(
        functools.partial(_ctx_attn_kernel, heads=heads, dh=dh),
        grid=(batch,),
        in_specs=[pl.BlockSpec((seq, D), lambda b: (b, 0)),
                  pl.BlockSpec((seq, D), lambda b: (b, 1)),
                  pl.BlockSpec((seq, D), lambda b: (b, 2))],
        out_specs=pl.BlockSpec((seq, D), lambda b: (b, 0)),
        out_shape=jax.ShapeDtypeStruct((batch * seq, D), BF16),
        compiler_params=_params(1, 32),
        name="context_attention",
    )(proj, proj, proj)


def _na_kernel(q_ref, k_ref, v_ref, kc_ref, vc_ref, *rest, hps, dh, rows):
    um_refs, (o_ref, vt_scr, vct_scr) = rest[:NA_BLOCKS_PER_STEP], rest[NA_BLOCKS_PER_STEP:]
    m = pl.program_id(2)
    win = NA_KROWS * GRID_W
    nkb = win // NA_KBLOCK
    hr = dh + NA_ONES_ROWS

    @pl.when(m == 0)
    def _():
        def fill(dst, vt):
            for h in range(hps):
                dst[h * hr:h * hr + dh, :] = vt[h * dh:(h + 1) * dh, :]
                dst[h * hr + dh:(h + 1) * hr, :] = jnp.ones((NA_ONES_ROWS, vt.shape[1]), BF16)

        for kb in range(v_ref.shape[0] // NA_KBLOCK):
            fill(vt_scr.at[kb], v_ref[kb * NA_KBLOCK:(kb + 1) * NA_KBLOCK, :].astype(F32).T.astype(BF16))
        fill(vct_scr, vc_ref[0].astype(F32).T.astype(BF16))

    qn = NA_QROWS * GRID_W
    units = [(sub, h) for sub in range(len(um_refs)) for h in range(hps)]
    kb0 = []
    scores = []
    for sub, um_ref in enumerate(um_refs):
        ws = jnp.clip((m * len(um_refs) + sub) * NA_QROWS - WIN_ROWS // 2, 0, rows - NA_KROWS)
        krows = pl.ds(pl.multiple_of(ws * GRID_W, NA_KBLOCK), win)
        kb0.append(ws // (NA_KBLOCK // GRID_W))
        for h in range(hps):
            cols = slice(h * dh, (h + 1) * dh)
            q = q_ref[sub * qn:(sub + 1) * qn, cols]
            s_loc = lax.dot_general(k_ref[krows, cols], q, _NT, preferred_element_type=F32) + um_ref[0, 0, h]
            s_ctx = lax.dot_general(kc_ref[0, :, cols], q, _NT, preferred_element_type=F32)
            scores.append((s_loc, s_ctx))
    probs = []
    for s_loc, s_ctx in scores:
        mx = jnp.maximum(jnp.max(s_loc, axis=0, keepdims=True), jnp.max(s_ctx, axis=0, keepdims=True))
        probs.append((jnp.exp2(s_loc - mx).astype(BF16), jnp.exp2(s_ctx - mx).astype(BF16)))
    for (sub, h), (p_loc, p_ctx) in zip(units, probs):
        hrows = slice(h * hr, (h + 1) * hr)
        ot = jnp.dot(vct_scr[hrows, :], p_ctx, preferred_element_type=F32)
        for t in range(nkb):
            ot = ot + jnp.dot(vt_scr[kb0[sub] + t, hrows, :], p_loc[t * NA_KBLOCK:(t + 1) * NA_KBLOCK, :],
                              preferred_element_type=F32)
        o_ref[sub * qn:(sub + 1) * qn, h * dh:(h + 1) * dh] = (ot[:dh] / ot[dh:dh + 1]).T.astype(BF16)


def _na_tables(rpb, rows):
    L, H = rpb.shape[:2]
    kh = WIN_ROWS
    col = np.arange(GRID_W)
    cs = np.clip(col - WIN_COLS // 2, 0, GRID_W - WIN_COLS)
    col_valid = (col[:, None] >= cs[None, :]) & (col[:, None] < cs[None, :] + WIN_COLS)
    n_rho = NA_KROWS + WIN_ROWS
    n_dr = 2 * WIN_ROWS - 1
    j = np.arange(NA_QROWS)
    flipped = jnp.pad(jnp.flip(rpb.astype(F32), axis=-1), ((0, 0), (0, 0), (0, 0), (GRID_W, GRID_W)))
    t1 = jnp.stack([flipped[..., GRID_W + WIN_COLS - 1 - kc:2 * GRID_W + WIN_COLS - 1 - kc] for kc in range(GRID_W)],
                   axis=3)
    t1 = jnp.where(col_valid, t1, -jnp.inf)
    lanes = []
    for jj in range(NA_QROWS):
        dr = np.clip(np.arange(n_rho) - WIN_ROWS - jj + WIN_ROWS - 1, 0, n_dr - 1)
        lo, hi = int((dr == 0).sum()) - 1, int((dr == n_dr - 1).sum()) - 1
        assert lo + n_dr + hi == n_rho
        lanes.append(jnp.concatenate([t1[:, :, :1]] * lo + [t1] + [t1[:, :, -1:]] * hi, axis=2))
    u = jnp.concatenate(lanes, axis=-1).reshape(L, H, n_rho * GRID_W, NA_QROWS * GRID_W)

    def band(r0, ws):
        rs = np.clip(r0 + j - kh // 2, 0, rows - kh)
        key_row = ws + np.arange(NA_KROWS)
        ok = (key_row[:, None] >= rs[None, :]) & (key_row[:, None] < rs[None, :] + kh)
        mk = np.where(ok, 0.0, -np.inf).astype(np.float32)
        return np.broadcast_to(mk[:, None, :, None], (NA_KROWS, GRID_W, NA_QROWS, GRID_W)).reshape(
            NA_KROWS * GRID_W, NA_QROWS * GRID_W)

    win = NA_KROWS * GRID_W
    kinds = []
    for r0, ws in ((0, 0), (NA_QROWS, 0), (rows - NA_QROWS, rows - NA_KROWS)):
        start = (ws - r0 + WIN_ROWS) * GRID_W
        kinds.append(u[:, :, start:start + win] * LOG2E + jnp.asarray(band(r0, ws)))
    return jnp.stack(kinds, axis=1)


def _neighbourhood_attention(proj, kc, vc, um, l, nc, dec_batch, dec_seq, D, heads, dh):
    rows = dec_seq // GRID_W
    nsub = NA_BLOCKS_PER_STEP
    assert rows % (NA_QROWS * nsub) == 0 and rows >= NA_KROWS
    hps = min(NA_HEADS_PER_STEP, heads)
    qn = nsub * NA_QROWS * GRID_W
    cw = hps * dh
    n_hg = heads // hps
    nblk = rows // NA_QROWS
    nstep = nblk // nsub
    q0 = nc // qn
    k0 = nc // dec_seq
    ncols = D // cw
    past = kc.shape[1]
    hr = dh + NA_ONES_ROWS
    kind = lambda blk: jnp.where(blk == 0, 0, jnp.where(blk == nblk - 1, 2, 1))

    def table_spec(sub):
        return pl.BlockSpec((1, 1, hps) + um.shape[3:], lambda b, g, m: (l, kind(m * nsub + sub), g, 0, 0))

    kernel = functools.partial(_na_kernel, hps=hps, dh=dh, rows=rows)
    return pl.pallas_call(
        kernel,
        grid=(dec_batch, n_hg, nstep),
        in_specs=[pl.BlockSpec((qn, cw), lambda b, g, m: (q0 + b * nstep + m, g)),
                  pl.BlockSpec((dec_seq, cw), lambda b, g, m: (k0 + b, ncols + g)),
                  pl.BlockSpec((dec_seq, cw), lambda b, g, m: (k0 + b, 2 * ncols + g)),
                  pl.BlockSpec((1, past, cw), lambda b, g, m: (b, 0, g)),
                  pl.BlockSpec((1, past, cw), lambda b, g, m: (b, 0, g))] + [table_spec(s) for s in range(nsub)],
        out_specs=pl.BlockSpec((qn, cw), lambda b, g, m: (b * nstep + m, g)),
        out_shape=jax.ShapeDtypeStruct((dec_batch * dec_seq, D), BF16),
        scratch_shapes=[pltpu.VMEM((dec_seq // NA_KBLOCK, hps * hr, NA_KBLOCK), BF16),
                        pltpu.VMEM((hps * hr, past), BF16)],
        compiler_params=_params(3, 56),
        name="neighbourhood_attention",
    )(proj, proj, proj, kc, vc, *([um] * nsub))


def _merge_kernel(ya_ref, ybc_ref, ybl_ref, ga_ref, gb_ref, wa_ref, wb_ref, o_ref, *, ncb):
    yb = jnp.where(pl.program_id(0) < ncb, ybc_ref[...], ybl_ref[...])
    a = jnp.dot(ya_ref[...], wa_ref[0], preferred_element_type=F32)
    b = jnp.dot(yb, wb_ref[0], preferred_element_type=F32)
    o = _sigmoid(ga_ref[...].astype(F32)) * a + _sigmoid(gb_ref[...].astype(F32)) * b
    o_ref[...] = o.astype(BF16)


def _merge(ya, yb_ctx, yb_lat, proj, w_pa, w_pb, l, D, tm):
    T = ya.shape[0]
    tn = _tile(D, 512)
    nj = D // tn
    ncb = yb_ctx.shape[0] // tm
    return pl.pallas_call(
        functools.partial(_merge_kernel, ncb=ncb),
        grid=(T // tm, nj),
        in_specs=[pl.BlockSpec((tm, D), lambda i, j: (i, 0)),
                  pl.BlockSpec((tm, D), lambda i, j: (jnp.minimum(i, ncb - 1), 0)),
                  pl.BlockSpec((tm, D), lambda i, j: (jnp.maximum(i - ncb, 0), 0)),
                  pl.BlockSpec((tm, tn), lambda i, j: (i, 5 * nj + j)),
                  pl.BlockSpec((tm, tn), lambda i, j: (i, 6 * nj + j)),
                  pl.BlockSpec((1, D, tn), lambda i, j: (l, 0, j)),
                  pl.BlockSpec((1, D, tn), lambda i, j: (l, 0, j))],
        out_specs=pl.BlockSpec((tm, tn), lambda i, j: (i, j)),
        out_shape=jax.ShapeDtypeStruct((T, D), BF16),
        compiler_params=_params(2, 56),
        name="gated_branch_merge",
    )(ya, yb_ctx, yb_lat, proj, proj, w_pa, w_pb)


def _argmax_first(vals):
    best = vals[0]
    idx = jnp.zeros(best.shape, I32)
    for k in range(1, len(vals)):
        upd = vals[k] > best
        idx = jnp.where(upd, k, idx)
        best = jnp.where(upd, vals[k], best)
    return idx


def _select(idx, vals):
    out = vals[-1]
    for k in range(len(vals) - 2, -1, -1):
        out = jnp.where(idx == k, vals[k], out)
    return out


def _route(logits_t, router_b):
    m = jnp.max(logits_t, axis=0, keepdims=True)
    ex = jnp.exp(logits_t - m)
    probs = ex / jnp.sum(ex, axis=0, keepdims=True)
    sel = probs + router_b
    sel_rows = [sel[e:e + 1, :] for e in range(N_EXPERTS)]
    prob_rows = [probs[e:e + 1, :] for e in range(N_EXPERTS)]
    scores = []
    for g in range(N_EXPERT_GROUPS):
        a, b, c, d = sel_rows[g * EXPERTS_PER_GROUP:(g + 1) * EXPERTS_PER_GROUP]
        hi1, lo1 = jnp.maximum(a, b), jnp.minimum(a, b)
        hi2, lo2 = jnp.maximum(c, d), jnp.minimum(c, d)
        top1 = jnp.maximum(hi1, hi2)
        top2 = jnp.maximum(jnp.minimum(hi1, hi2), jnp.where(hi1 >= hi2, lo1, lo2))
        scores.append(top1 + top2)
    best_g = _argmax_first(scores)
    in_sel = [_select(best_g, [sel_rows[g * EXPERTS_PER_GROUP + k] for g in range(N_EXPERT_GROUPS)])
              for k in range(EXPERTS_PER_GROUP)]
    in_prob = [_select(best_g, [prob_rows[g * EXPERTS_PER_GROUP + k] for g in range(N_EXPERT_GROUPS)])
               for k in range(EXPERTS_PER_GROUP)]
    i1 = _argmax_first(in_sel)
    i2 = _argmax_first([jnp.where(i1 == k, -jnp.inf, in_sel[k]) for k in range(EXPERTS_PER_GROUP)])
    w1 = _select(i1, in_prob)
    w2 = _select(i2, in_prob)
    tot = w1 + w2
    base = best_g * EXPERTS_PER_GROUP
    return (base + i1, base + i2), (w1 / tot, w2 / tot)


def _out_kernel(mg_ref, x_ref, mod_ref, g_ref, w_ref, rw_ref, rb_ref, xo_ref, h_ref, e_ref, gt_ref, acc0, acc1):
    s = pl.program_id(0)

    @pl.when(s == 0)
    def _():
        acc1[...] = jnp.zeros(acc1.shape, acc1.dtype)

    def step(acc_w, acc_r):
        gate1 = mod_ref[0, 0, 2:3, :]
        shift2 = mod_ref[0, 0, 3:4, :]
        scale2 = 1.0 + mod_ref[0, 0, 4:5, :]
        g = g_ref[0]
        acc_w[...] = jnp.dot(mg_ref[...], w_ref[0], preferred_element_type=F32)
        for c in range(x_ref.shape[0] // ROW_CHUNK):
            rows = slice(c * ROW_CHUNK, (c + 1) * ROW_CHUNK)
            xn = x_ref[rows, :] + gate1 * acc_r[rows, :]
            xo_ref[rows, :] = xn
            h_ref[rows, :] = (_rms_scale(xn) * g * scale2 + shift2).astype(BF16)
        logits_t = lax.dot_general(rw_ref[...], h_ref[...], _NT, preferred_element_type=F32)
        experts, gates = _route(logits_t, rb_ref[...])
        for k in range(TOP_K):
            e_ref[k:k + 1, :] = experts[k]
            gt_ref[k:k + 1, :] = gates[k]

    pl.when(s % 2 == 0)(lambda: step(acc0, acc1))
    pl.when(s % 2 == 1)(lambda: step(acc1, acc0))


def _output_projection(merged, x, mod, norm_g, w_out, router_wt, router_b, l, cond_of_block, tm):
    T, D = x.shape
    E = router_wt.shape[0]
    n = T // tm
    cur = lambda s: jnp.minimum(s, n - 1)
    prev = lambda s: jnp.maximum(s - 1, 0)
    return pl.pallas_call(
        _out_kernel,
        grid=(n + 1,),
        in_specs=[pl.BlockSpec((tm, D), lambda s: (cur(s), 0)),
                  pl.BlockSpec((tm, D), lambda s: (prev(s), 0)),
                  pl.BlockSpec((1, 1, N_MOD, D), lambda s: (l, cond_of_block(prev(s)), 0, 0)),
                  pl.BlockSpec((1, 1, D), lambda s: (l, 0, 0)),
                  pl.BlockSpec((1, D, D), lambda s: (l, 0, 0)),
                  pl.BlockSpec((E, D), lambda s: (0, 0)),
                  pl.BlockSpec((E, 1), lambda s: (0, 0))],
        out_specs=[pl.BlockSpec((tm, D), lambda s: (prev(s), 0)),
                   pl.BlockSpec((tm, D), lambda s: (prev(s), 0)),
                   pl.BlockSpec((TOP_K, tm), lambda s: (0, prev(s))),
                   pl.BlockSpec((TOP_K, tm), lambda s: (0, prev(s)))],
        out_shape=[jax.ShapeDtypeStruct((T, D), F32),
                   jax.ShapeDtypeStruct((T, D), BF16),
                   jax.ShapeDtypeStruct((TOP_K, T), I32),
                   jax.ShapeDtypeStruct((TOP_K, T), F32)],
        scratch_shapes=[pltpu.VMEM((tm, D), F32), pltpu.VMEM((tm, D), F32)],
        compiler_params=_params(1, 56),
        name="output_projection_norm2_router",
    )(merged, x, mod, norm_g.reshape(-1, 1, D), w_out, router_wt, router_b.reshape(E, 1))


def _moe_kernel(blk_ref, exp_ref, lo_ref, hi_ref, x_ref, wg_ref, wu_ref, wd_ref, o_ref, wg_b, wu_b, wd_b):
    w = pl.program_id(0)
    lo, hi = lo_ref[w], hi_ref[w]
    row0 = blk_ref[w] * MOE_BLOCK

    @pl.when((w == 0) | (exp_ref[w] != exp_ref[jnp.maximum(w - 1, 0)]))
    def _():
        for src, dst in ((wg_ref, wg_b), (wu_ref, wu_b), (wd_ref, wd_b)):
            def chunk(rows, src=src, dst=dst):
                dst[rows, :] = src[0, rows, :].astype(BF16)

            _for_chunks(dst.shape[0], CAST_CHUNK, chunk)

    @pl.when(lo == row0)
    def _():
        o_ref[...] = jnp.zeros(o_ref.shape, o_ref.dtype)

    @pl.when(hi > lo)
    def _():
        x = x_ref[...]
        g = jnp.dot(x, wg_b[...], preferred_element_type=F32)
        u = jnp.dot(x, wu_b[...], preferred_element_type=F32)
        hid = (g * _sigmoid(g)) * u
        y = jnp.dot(hid.astype(BF16), wd_b[...], preferred_element_type=F32)
        row = row0 + lax.broadcasted_iota(I32, (MOE_BLOCK, 1), 0)
        o_ref[...] += jnp.where((row >= lo) & (row < hi), y, 0.0).astype(o_ref.dtype)


def _moe_experts(xs, items, w_gate, w_up, w_down, l):
    A, D = xs.shape
    E, DE = w_gate.shape[1], w_gate.shape[3]
    w_gate = w_gate.reshape(-1, D, DE)
    w_up = w_up.reshape(-1, D, DE)
    w_down = w_down.reshape(-1, DE, D)
    n_items = items[0].shape[0]
    expert = lambda w, blk, ex, lo, hi: (l * E + ex[w], 0, 0)
    grid_spec = pltpu.PrefetchScalarGridSpec(
        num_scalar_prefetch=4,
        grid=(n_items,),
        in_specs=[pl.BlockSpec((MOE_BLOCK, D), lambda w, blk, ex, lo, hi: (blk[w], 0)),
                  pl.BlockSpec((1, D, DE), expert, pipeline_mode=pl.Buffered(1)),
                  pl.BlockSpec((1, D, DE), expert, pipeline_mode=pl.Buffered(1)),
                  pl.BlockSpec((1, DE, D), expert)],
        out_specs=pl.BlockSpec((MOE_BLOCK, D), lambda w, blk, ex, lo, hi: (blk[w], 0)),
        scratch_shapes=[pltpu.VMEM((D, DE), BF16), pltpu.VMEM((D, DE), BF16), pltpu.VMEM((DE, D), BF16)],
    )
    return pl.pallas_call(
        _moe_kernel,
        grid_spec=grid_spec,
        out_shape=jax.ShapeDtypeStruct((A, D), BF16),
        compiler_params=_params(1, 58),
        name="moe_experts",
    )(*items, xs, w_gate, w_up, w_down)


def _dispatch_plan(experts, T):
    A = TOP_K * T
    nb = A // MOE_BLOCK
    n_items = nb + N_EXPERTS - 1
    e_flat = experts.reshape(A)
    iota = jnp.arange(A, dtype=I32)
    _, order = lax.sort((e_flat, iota), num_keys=1, is_stable=True)
    _, rank = lax.sort((order, iota), num_keys=1)
    eid = jnp.arange(N_EXPERTS, dtype=I32)
    counts = jnp.sum((e_flat[:, None] == eid[None, :]).astype(I32), axis=0)
    ends = jnp.cumsum(counts)
    starts = ends - counts
    first_blk = starts // MOE_BLOCK
    n_blk = jnp.where(counts > 0, (ends - 1) // MOE_BLOCK - first_blk + 1, 0)
    item_end = jnp.cumsum(n_blk)
    item_start = item_end - n_blk
    w = jnp.arange(n_items, dtype=I32)
    valid = w < item_end[-1]
    ex = jnp.minimum(jnp.sum((w[:, None] >= item_end[None, :]).astype(I32), axis=1), N_EXPERTS - 1)
    pick = lambda tbl: jnp.sum(jnp.where(ex[:, None] == eid[None, :], tbl[None, :], 0), axis=1)
    blk = jnp.where(valid, pick(first_blk) + w - pick(item_start), nb - 1)
    lo = jnp.where(valid, jnp.maximum(pick(starts), blk * MOE_BLOCK), 0)
    hi = jnp.where(valid, jnp.minimum(pick(ends), (blk + 1) * MOE_BLOCK), 0)
    ex = jnp.where(valid, ex, jnp.max(jnp.where(counts > 0, eid, 0)))
    return order % T, rank, (blk.astype(I32), ex.astype(I32), lo.astype(I32), hi.astype(I32))


def _combined(x_ref, y1_ref, y2_ref, g1_ref, g2_ref, gate2, rows):
    y = g1_ref[rows, :] * y1_ref[rows, :].astype(F32) + g2_ref[rows, :] * y2_ref[rows, :].astype(F32)
    return x_ref[rows, :] + gate2 * y


def _combine_kernel(x_ref, y1_ref, y2_ref, g1_ref, g2_ref, mod_ref, modn_ref, gn_ref, xo_ref, h_ref):
    gate2 = mod_ref[0, 0, 5:6, :]
    scale1 = 1.0 + modn_ref[0, 0, 1:2, :]
    shift1 = modn_ref[0, 0, 0:1, :]
    g = gn_ref[0]

    def chunk(rows):
        xn = _combined(x_ref, y1_ref, y2_ref, g1_ref, g2_ref, gate2, rows)
        xo_ref[rows, :] = xn
        h_ref[rows, :] = (_rms_scale(xn) * g * scale1 + shift1).astype(BF16)

    _for_chunks(x_ref.shape[0], ROW_CHUNK, chunk)


def _combine_final_kernel(x_ref, y1_ref, y2_ref, g1_ref, g2_ref, mod_ref, gf_ref, yp_ref, ys_ref, *, ncb):
    i = pl.program_id(0)
    gate2 = mod_ref[0, 0, 5:6, :]
    g = gf_ref[...]

    def run(dst_ref):
        def chunk(rows):
            xn = _combined(x_ref, y1_ref, y2_ref, g1_ref, g2_ref, gate2, rows)
            dst_ref[rows, :] = _rms_scale(xn) * g

        _for_chunks(x_ref.shape[0], ROW_CHUNK, chunk)

    pl.when(i < ncb)(lambda: run(yp_ref))
    pl.when(i >= ncb)(lambda: run(ys_ref))


def _combine(x, y1, y2, g1, g2, mod, norm1_g, l, cond_of_block, tm):
    T, D = x.shape
    row = pl.BlockSpec((tm, D), lambda i: (i, 0))
    col = pl.BlockSpec((tm, 1), lambda i: (i, 0))
    return pl.pallas_call(
        _combine_kernel,
        grid=(T // tm,),
        in_specs=[row, row, row, col, col,
                  pl.BlockSpec((1, 1, N_MOD, D), lambda i: (l, cond_of_block(i), 0, 0)),
                  pl.BlockSpec((1, 1, N_MOD, D), lambda i: (l + 1, cond_of_block(i), 0, 0)),
                  pl.BlockSpec((1, 1, D), lambda i: (l + 1, 0, 0))],
        out_specs=[row, row],
        out_shape=[jax.ShapeDtypeStruct((T, D), F32), jax.ShapeDtypeStruct((T, D), BF16)],
        compiler_params=_params(1, 52),
        name="moe_combine_residual_norm1",
    )(x, y1, y2, g1, g2, mod, mod, norm1_g.reshape(-1, 1, D))


def _combine_final(x, y1, y2, g1, g2, mod, final_g, l, nc, cond_of_block, tm):
    T, D = x.shape
    ncb = nc // tm
    row = pl.BlockSpec((tm, D), lambda i: (i, 0))
    col = pl.BlockSpec((tm, 1), lambda i: (i, 0))
    return pl.pallas_call(
        functools.partial(_combine_final_kernel, ncb=ncb),
        grid=(T // tm,),
        in_specs=[row, row, row, col, col,
                  pl.BlockSpec((1, 1, N_MOD, D), lambda i: (l, cond_of_block(i), 0, 0)),
                  pl.BlockSpec((1, D), lambda i: (0, 0))],
        out_specs=[pl.BlockSpec((tm, D), lambda i: (jnp.minimum(i, ncb - 1), 0)),
                   pl.BlockSpec((tm, D), lambda i: (jnp.maximum(i - ncb, 0), 0))],
        out_shape=[jax.ShapeDtypeStruct((nc, D), F32), jax.ShapeDtypeStruct((T - nc, D), F32)],
        compiler_params=_params(1, 52),
        name="moe_combine_residual_final_norm",
    )(x, y1, y2, g1, g2, mod, final_g.reshape(1, D))


def kernel(x_prompt, x_sample, cache_k, cache_v, c, c_ctx, w_ada, b_ada, norm1_g, w_in, gmlp_g, w_s, b_s, rpb,
           w_proj_a, w_proj_b, w_out, norm2_g, router_w, router_b, w_gate, w_up, w_down, final_g):
    batch, seq, D = x_prompt.shape
    dec_batch, dec_seq, _ = x_sample.shape
    depth = w_in.shape[0]
    heads, dh = cache_k.shape[3], cache_k.shape[4]
    past = cache_k.shape[2]
    nc, nl = batch * seq, dec_batch * dec_seq
    T = nc + nl
    assert heads * dh == D and gmlp_g.shape[1] == D and w_in.shape[2] == 7 * D
    assert 1 + dec_batch <= COND_ROWS
    assert router_w.shape[1] == N_EXPERTS
    assert seq % GMLP_CHUNK == 0 and dec_seq % GMLP_CHUNK == 0 and dec_seq % GRID_W == 0
    assert nc % dec_seq == 0 and (TOP_K * T) % MOE_BLOCK == 0

    tm_big = 1024 if (nc % 1024 == 0 and dec_seq % 1024 == 0) else GMLP_CHUNK
    tm_mid = 512 if (nc % 512 == 0 and dec_seq % 512 == 0) else GMLP_CHUNK

    def cond_of(tm):
        ncb, per = nc // tm, dec_seq // tm
        return lambda i: jnp.where(i < ncb, 0, 1 + (i - ncb) // per)

    cond = jnp.zeros((COND_ROWS, D), F32).at[0].set(c_ctx).at[1:1 + dec_batch].set(c)
    mod = _modulation_all(cond, w_ada, b_ada)

    w_s_b = w_s.astype(BF16)
    w_pa_b = w_proj_a.astype(BF16)
    w_pb_b = w_proj_b.astype(BF16)
    w_out_b = w_out.astype(BF16)
    kc_all = cache_k.astype(BF16).reshape(dec_batch, depth, past, D)
    vc_all = cache_v.astype(BF16).reshape(dec_batch, depth, past, D)
    router_wt = router_w.T.astype(BF16)
    na_tables = _na_tables(rpb, dec_seq // GRID_W)

    x, h = _prologue(x_prompt.reshape(nc, D), x_sample.reshape(nl, D), mod, norm1_g, cond_of(tm_mid), tm_mid)
    new_k = jnp.zeros((batch, depth, seq, D), F32)
    new_v = jnp.zeros((batch, depth, seq, D), F32)
    for l in range(depth):
        proj, new_k, new_v = _input_projection(h, w_in, new_k, new_v, l, tm_big, dh ** -0.5 * LOG2E)
        ya = _gmlp(proj, gmlp_g, w_s_b, b_s, l, D, tm_mid)
        yb_ctx = _context_attention(proj, batch, seq, D, heads, dh)
        yb_lat = _neighbourhood_attention(proj, kc_all[:, l], vc_all[:, l], na_tables, l, nc, dec_batch,
                                          dec_seq, D, heads, dh)
        merged = _merge(ya, yb_ctx, yb_lat, proj, w_pa_b, w_pb_b, l, D, tm_big)
        x, h2, experts, gates = _output_projection(merged, x, mod, norm2_g, w_out_b, router_wt, router_b, l,
                                                   cond_of(tm_mid), tm_mid)
        tok_sorted, rank, items = _dispatch_plan(experts, T)
        yexp = _moe_experts(h2[tok_sorted], items, w_gate, w_up, w_down, l)
        y1, y2, g1, g2 = yexp[rank[:T]], yexp[rank[T:]], gates[0][:, None], gates[1][:, None]
        if l + 1 < depth:
            x, h = _combine(x, y1, y2, g1, g2, mod, norm1_g, l, cond_of(tm_mid), tm_mid)
        else:
            y_prompt, y_sample = _combine_final(x, y1, y2, g1, g2, mod, final_g, l, nc, cond_of(tm_mid), tm_mid)

    return (y_prompt.reshape(batch, seq, D), y_sample.reshape(dec_batch, dec_seq, D),
            new_k.reshape(batch, depth, seq, heads, dh), new_v.reshape(batch, depth, seq, heads, dh))
```

```python
import functools

import numpy as np
import jax
import jax.numpy as jnp
from jax import lax
from jax.experimental import pallas as pl
from jax.experimental.pallas import tpu as pltpu

F32 = jnp.float32
BF16 = jnp.bfloat16
I32 = jnp.int32

EPS = 1e-6
N_MOD = 6
GRID_W = 64
WIN_ROWS = 8
WIN_COLS = 16
GMLP_CHUNK = 128
GMLP_GROUPS = 8
N_EXPERTS = 16
N_EXPERT_GROUPS = 4
EXPERTS_PER_GROUP = N_EXPERTS // N_EXPERT_GROUPS
TOP_K = 2
MOE_BLOCK = 256
COND_ROWS = 8
ROW_CHUNK = 64
CAST_CHUNK = 256
NA_QROWS = 4
NA_KROWS = 12
NA_KBLOCK = 256
NA_ONES_ROWS = 16
NA_HEADS_PER_STEP = 4
NA_BLOCKS_PER_STEP = 1
LOG2E = 1.4426950408889634
MIB = 1024 * 1024

_NT = (((1,), (1,)), ((), ()))


def _tile(n, pref):
    return pref if n % pref == 0 else n


def _params(n_axes, vmem_mib):
    return pltpu.CompilerParams(dimension_semantics=("arbitrary",) * n_axes,
                                vmem_limit_bytes=vmem_mib * MIB)


def _for_chunks(n, c, fn):
    c = min(c, n)

    def body(t, carry):
        fn(pl.ds(pl.multiple_of(t * c, c), c))
        return carry

    lax.fori_loop(0, n // c, body, 0)


def _sigmoid(x):
    return 1.0 / (1.0 + jnp.exp(-x))


def _gelu_tanh(x):
    return x * (0.5 * (1.0 + jnp.tanh(0.7978845608028654 * (x + 0.044715 * (x * x * x)))))


def _rms_scale(x):
    return x * lax.rsqrt(jnp.mean(x * x, axis=-1, keepdims=True) + EPS)


def _ada_kernel(c_ref, w_ref, b_ref, o_ref):
    c = c_ref[...]
    s = (c * _sigmoid(c)).astype(BF16)
    o_ref[0] = jnp.dot(s, w_ref[0].astype(BF16), preferred_element_type=F32) + b_ref[0]


def _modulation_all(cond, w_ada, b_ada):
    L, D, N = w_ada.shape
    tn = _tile(N, 1024)
    out = pl.pallas_call(
        _ada_kernel,
        grid=(L, N // tn),
        in_specs=[pl.BlockSpec((COND_ROWS, D), lambda l, j: (0, 0)),
                  pl.BlockSpec((1, D, tn), lambda l, j: (l, 0, j)),
                  pl.BlockSpec((1, 1, tn), lambda l, j: (l, 0, j))],
        out_specs=pl.BlockSpec((1, COND_ROWS, tn), lambda l, j: (l, 0, j)),
        out_shape=jax.ShapeDtypeStruct((L, COND_ROWS, N), F32),
        compiler_params=_params(2, 40),
        name="ada_modulation",
    )(cond, w_ada, b_ada.reshape(L, 1, N))
    return out.reshape(L, COND_ROWS, N_MOD, D)


def _prologue_kernel(xp_ref, xs_ref, mod_ref, g_ref, x_ref, h_ref, *, ncb):
    i = pl.program_id(0)
    g = g_ref[0]
    scale1 = 1.0 + mod_ref[0, 0, 1:2, :]
    shift1 = mod_ref[0, 0, 0:1, :]

    def run(src_ref):
        def chunk(rows):
            x = src_ref[rows, :]
            x_ref[rows, :] = x
            h_ref[rows, :] = (_rms_scale(x) * g * scale1 + shift1).astype(BF16)

        _for_chunks(x_ref.shape[0], ROW_CHUNK, chunk)

    pl.when(i < ncb)(lambda: run(xp_ref))
    pl.when(i >= ncb)(lambda: run(xs_ref))


def _prologue(xp, xs, mod, norm_g, cond_of_block, tm):
    nc, D = xp.shape
    T = nc + xs.shape[0]
    ncb = nc // tm
    row = pl.BlockSpec((tm, D), lambda i: (i, 0))
    return pl.pallas_call(
        functools.partial(_prologue_kernel, ncb=ncb),
        grid=(T // tm,),
        in_specs=[pl.BlockSpec((tm, D), lambda i: (jnp.minimum(i, ncb - 1), 0)),
                  pl.BlockSpec((tm, D), lambda i: (jnp.maximum(i - ncb, 0), 0)),
                  pl.BlockSpec((1, 1, N_MOD, D), lambda i: (0, cond_of_block(i), 0, 0)),
                  pl.BlockSpec((1, 1, D), lambda i: (0, 0, 0))],
        out_specs=[row, row],
        out_shape=[jax.ShapeDtypeStruct((T, D), F32), jax.ShapeDtypeStruct((T, D), BF16)],
        compiler_params=_params(1, 40),
        name="concat_norm1",
    )(xp, xs, mod, norm_g.reshape(-1, 1, D))


def _proj_kernel(h_ref, w_ref, nk_in, nv_in, o_ref, k_ref, v_ref, w_scr, *, ncb, jk0, nk, qscale):
    del nk_in, nv_in
    j = pl.program_id(0)
    i = pl.program_id(1)

    @pl.when(i == 0)
    def _():
        def chunk(rows):
            w_scr[rows, :] = w_ref[0, rows, :].astype(BF16)

        _for_chunks(w_scr.shape[0], CAST_CHUNK, chunk)

    acc = jnp.dot(h_ref[...], w_scr[...], preferred_element_type=F32)
    o_ref[...] = (acc * jnp.where(j < jk0, qscale, 1.0)).astype(BF16)

    def put(dst_ref):
        seq = dst_ref.shape[2]
        for b in range(dst_ref.shape[0]):
            dst_ref[b, 0] = acc[b * seq:(b + 1) * seq, :]

    pl.when((i < ncb) & (j >= jk0) & (j < jk0 + nk))(lambda: put(k_ref))
    pl.when((i < ncb) & (j >= jk0 + nk) & (j < jk0 + 2 * nk))(lambda: put(v_ref))


def _input_projection(h, w_in, new_k, new_v, l, tm, qscale):
    T, D = h.shape
    N = w_in.shape[-1]
    batch, _, seq, _ = new_k.shape
    tn = _tile(D, 1024)
    ncb = batch * seq // tm
    jk0, nk = D // tn, D // tn
    kernel = functools.partial(_proj_kernel, ncb=ncb, jk0=jk0, nk=nk, qscale=qscale)

    def cache_block(j0):
        def index(j, i):
            inside = (j >= j0) & (j < j0 + nk)
            row = jnp.where(j < j0, 0, jnp.where(inside, jnp.minimum(i, ncb - 1), ncb - 1))
            col = jnp.where(j < j0, 0, jnp.where(inside, j - j0, nk - 1))
            return (row, l, 0, col)
        return pl.BlockSpec((tm // seq, 1, seq, tn), index)

    return pl.pallas_call(
        kernel,
        grid=(N // tn, T // tm),
        in_specs=[pl.BlockSpec((tm, D), lambda j, i: (i, 0)),
                  pl.BlockSpec((1, D, tn), lambda j, i: (l, 0, j)),
                  pl.BlockSpec(memory_space=pl.ANY),
                  pl.BlockSpec(memory_space=pl.ANY)],
        out_specs=[pl.BlockSpec((tm, tn), lambda j, i: (i, j)), cache_block(jk0), cache_block(jk0 + nk)],
        out_shape=[jax.ShapeDtypeStruct((T, N), BF16),
                   jax.ShapeDtypeStruct(new_k.shape, F32),
                   jax.ShapeDtypeStruct(new_v.shape, F32)],
        scratch_shapes=[pltpu.VMEM((D, tn), BF16)],
        input_output_aliases={2: 1, 3: 2},
        compiler_params=_params(2, 60),
        name="input_projection",
    )(h, w_in, new_k, new_v)


def _gmlp_kernel(u_ref, gv_ref, g_ref, ws_ref, bs_ref, o_ref):
    gw = u_ref.shape[1] // GMLP_GROUPS
    g = g_ref[0]

    def chunk(rows):
        v = _rms_scale(_gelu_tanh(gv_ref[rows, :].astype(F32))) * g
        v = v.astype(BF16)
        for grp in range(GMLP_GROUPS):
            cols = slice(grp * gw, (grp + 1) * gw)
            s = jnp.dot(ws_ref[grp], v[:, cols], preferred_element_type=F32) + bs_ref[grp]
            u = _gelu_tanh(u_ref[rows, cols].astype(F32))
            o_ref[rows, cols] = (u * s).astype(BF16)

    _for_chunks(u_ref.shape[0], GMLP_CHUNK, chunk)


def _gmlp(proj, gmlp_g, w_s, b_s, l, D, tm):
    T = proj.shape[0]
    gw = D // GMLP_GROUPS
    bs = jnp.broadcast_to(b_s[l][:, :, None], (GMLP_GROUPS, GMLP_CHUNK, gw))
    return pl.pallas_call(
        _gmlp_kernel,
        grid=(T // tm,),
        in_specs=[pl.BlockSpec((tm, D), lambda i: (i, 3)),
                  pl.BlockSpec((tm, D), lambda i: (i, 4)),
                  pl.BlockSpec((1, 1, D), lambda i: (l, 0, 0)),
                  pl.BlockSpec((None, GMLP_GROUPS, GMLP_CHUNK, GMLP_CHUNK), lambda i: (l, 0, 0, 0)),
                  pl.BlockSpec((GMLP_GROUPS, GMLP_CHUNK, gw), lambda i: (0, 0, 0))],
        out_specs=pl.BlockSpec((tm, D), lambda i: (i, 0)),
        out_shape=jax.ShapeDtypeStruct((T, D), BF16),
        compiler_params=_params(1, 40),
        name="gmlp_spatial_gating",
    )(proj, proj, gmlp_g.reshape(-1, 1, D), w_s, bs)


def _ctx_attn_kernel(q_ref, k_ref, v_ref, o_ref, *, heads, dh):
    for h in range(heads):
        cols = slice(h * dh, (h + 1) * dh)
        s = lax.dot_general(q_ref[:, cols], k_ref[:, cols], _NT, preferred_element_type=F32)
        p = jnp.exp2(s - jnp.max(s, axis=-1, keepdims=True))
        denom = jnp.sum(p, axis=-1, keepdims=True)
        o = jnp.dot(p.astype(BF16), v_ref[:, cols], preferred_element_type=F32)
        o_ref[:, cols] = (o / denom).astype(BF16)


def _context_attention(proj, batch, seq, D, heads, dh):
    return pl.pallas_call(
        functools.partial(_ctx_attn_kernel, heads=heads, dh=dh),
        grid=(batch,),
        in_specs=[pl.BlockSpec((seq, D), lambda b: (b, 0)),
                  pl.BlockSpec((seq, D), lambda b: (b, 1)),
                  pl.BlockSpec((seq, D), lambda b: (b, 2))],
        out_specs=pl.BlockSpec((seq, D), lambda b: (b, 0)),
        out_shape=jax.ShapeDtypeStruct((batch * seq, D), BF16),
        compiler_params=_params(1, 32),
        name="context_attention",
    )(proj, proj, proj)


def _na_kernel(q_ref, k_ref, v_ref, kc_ref, vc_ref, *rest, hps, dh, rows):
    um_refs, (o_ref, vt_scr, vct_scr) = rest[:NA_BLOCKS_PER_STEP], rest[NA_BLOCKS_PER_STEP:]
    m = pl.program_id(2)
    win = NA_KROWS * GRID_W
    nkb = win // NA_KBLOCK
    hr = dh + NA_ONES_ROWS

    @pl.when(m == 0)
    def _():
        def fill(dst, vt):
            for h in range(hps):
                dst[h * hr:h * hr + dh, :] = vt[h * dh:(h + 1) * dh, :]
                dst[h * hr + dh:(h + 1) * hr, :] = jnp.ones((NA_ONES_ROWS, vt.shape[1]), BF16)

        for kb in range(v_ref.shape[0] // NA_KBLOCK):
            fill(vt_scr.at[kb], v_ref[kb * NA_KBLOCK:(kb + 1) * NA_KBLOCK, :].astype(F32).T.astype(BF16))
        fill(vct_scr, vc_ref[0].astype(F32).T.astype(BF16))

    qn = NA_QROWS * GRID_W
    units = [(sub, h) for sub in range(len(um_refs)) for h in range(hps)]
    kb0 = []
    scores = []
    for sub, um_ref in enumerate(um_refs):
        ws = jnp.clip((m * len(um_refs) + sub) * NA_QROWS - WIN_ROWS // 2, 0, rows - NA_KROWS)
        krows = pl.ds(pl.multiple_of(ws * GRID_W, NA_KBLOCK), win)
        kb0.append(ws // (NA_KBLOCK // GRID_W))
        for h in range(hps):
            cols = slice(h * dh, (h + 1) * dh)
            q = q_ref[sub * qn:(sub + 1) * qn, cols]
            s_loc = lax.dot_general(k_ref[krows, cols], q, _NT, preferred_element_type=F32) + um_ref[0, 0, h]
            s_ctx = lax.dot_general(kc_ref[0, :, cols], q, _NT, preferred_element_type=F32)
            scores.append((s_loc, s_ctx))
    probs = []
    for s_loc, s_ctx in scores:
        mx = jnp.maximum(jnp.max(s_loc, axis=0, keepdims=True), jnp.max(s_ctx, axis=0, keepdims=True))
        probs.append((jnp.exp2(s_loc - mx).astype(BF16), jnp.exp2(s_ctx - mx).astype(BF16)))
    for (sub, h), (p_loc, p_ctx) in zip(units, probs):
        hrows = slice(h * hr, (h + 1) * hr)
        ot = jnp.dot(vct_scr[hrows, :], p_ctx, preferred_element_type=F32)
        for t in range(nkb):
            ot = ot + jnp.dot(vt_scr[kb0[sub] + t, hrows, :], p_loc[t * NA_KBLOCK:(t + 1) * NA_KBLOCK, :],
                              preferred_element_type=F32)
        o_ref[sub * qn:(sub + 1) * qn, h * dh:(h + 1) * dh] = (ot[:dh] / ot[dh:dh + 1]).T.astype(BF16)


def _na_tables(rpb, rows):
    L, H = rpb.shape[:2]
    kh = WIN_ROWS
    col = np.arange(GRID_W)
    cs = np.clip(col - WIN_COLS // 2, 0, GRID_W - WIN_COLS)
    col_valid = (col[:, None] >= cs[None, :]) & (col[:, None] < cs[None, :] + WIN_COLS)
    n_rho = NA_KROWS + WIN_ROWS
    n_dr = 2 * WIN_ROWS - 1
    j = np.arange(NA_QROWS)
    flipped = jnp.pad(jnp.flip(rpb.astype(F32), axis=-1), ((0, 0), (0, 0), (0, 0), (GRID_W, GRID_W)))
    t1 = jnp.stack([flipped[..., GRID_W + WIN_COLS - 1 - kc:2 * GRID_W + WIN_COLS - 1 - kc] for kc in range(GRID_W)],
                   axis=3)
    t1 = jnp.where(col_valid, t1, -jnp.inf)
    lanes = []
    for jj in range(NA_QROWS):
        dr = np.clip(np.arange(n_rho) - WIN_ROWS - jj + WIN_ROWS - 1, 0, n_dr - 1)
        lo, hi = int((dr == 0).sum()) - 1, int((dr == n_dr - 1).sum()) - 1
        assert lo + n_dr + hi == n_rho
        lanes.append(jnp.concatenate([t1[:, :, :1]] * lo + [t1] + [t1[:, :, -1:]] * hi, axis=2))
    u = jnp.concatenate(lanes, axis=-1).reshape(L, H, n_rho * GRID_W, NA_QROWS * GRID_W)

    def band(r0, ws):
        rs = np.clip(r0 + j - kh // 2, 0, rows - kh)
        key_row = ws + np.arange(NA_KROWS)
        ok = (key_row[:, None] >= rs[None, :]) & (key_row[:, None] < rs[None, :] + kh)
        mk = np.where(ok, 0.0, -np.inf).astype(np.float32)
        return np.broadcast_to(mk[:, None, :, None], (NA_KROWS, GRID_W, NA_QROWS, GRID_W)).reshape(
            NA_KROWS * GRID_W, NA_QROWS * GRID_W)

    win = NA_KROWS * GRID_W
    kinds = []
    for r0, ws in ((0, 0), (NA_QROWS, 0), (rows - NA_QROWS, rows - NA_KROWS)):
        start = (ws - r0 + WIN_ROWS) * GRID_W
        kinds.append(u[:, :, start:start + win] * LOG2E + jnp.asarray(band(r0, ws)))
    return jnp.stack(kinds, axis=1)


def _neighbourhood_attention(proj, kc, vc, um, l, nc, dec_batch, dec_seq, D, heads, dh):
    rows = dec_seq // GRID_W
    nsub = NA_BLOCKS_PER_STEP
    assert rows % (NA_QROWS * nsub) == 0 and rows >= NA_KROWS
    hps = min(NA_HEADS_PER_STEP, heads)
    qn = nsub * NA_QROWS * GRID_W
    cw = hps * dh
    n_hg = heads // hps
    nblk = rows // NA_QROWS
    nstep = nblk // nsub
    q0 = nc // qn
    k0 = nc // dec_seq
    ncols = D // cw
    past = kc.shape[1]
    hr = dh + NA_ONES_ROWS
    kind = lambda blk: jnp.where(blk == 0, 0, jnp.where(blk == nblk - 1, 2, 1))

    def table_spec(sub):
        return pl.BlockSpec((1, 1, hps) + um.shape[3:], lambda b, g, m: (l, kind(m * nsub + sub), g, 0, 0))

    kernel = functools.partial(_na_kernel, hps=hps, dh=dh, rows=rows)
    return pl.pallas_call(
        kernel,
        grid=(dec_batch, n_hg, nstep),
        in_specs=[pl.BlockSpec((qn, cw), lambda b, g, m: (q0 + b * nstep + m, g)),
                  pl.BlockSpec((dec_seq, cw), lambda b, g, m: (k0 + b, ncols + g)),
                  pl.BlockSpec((dec_seq, cw), lambda b, g, m: (k0 + b, 2 * ncols + g)),
                  pl.BlockSpec((1, past, cw), lambda b, g, m: (b, 0, g)),
                  pl.BlockSpec((1, past, cw), lambda b, g, m: (b, 0, g))] + [table_spec(s) for s in range(nsub)],
        out_specs=pl.BlockSpec((qn, cw), lambda b, g, m: (b * nstep + m, g)),
        out_shape=jax.ShapeDtypeStruct((dec_batch * dec_seq, D), BF16),
        scratch_shapes=[pltpu.VMEM((dec_seq // NA_KBLOCK, hps * hr, NA_KBLOCK), BF16),
                        pltpu.VMEM((hps * hr, past), BF16)],
        compiler_params=_params(3, 56),
        name="neighbourhood_attention",
    )(proj, proj, proj, kc, vc, *([um] * nsub))


def _merge_kernel(ya_ref, ybc_ref, ybl_ref, ga_ref, gb_ref, wa_ref, wb_ref, o_ref, *, ncb):
    yb = jnp.where(pl.program_id(0) < ncb, ybc_ref[...], ybl_ref[...])
    a = jnp.dot(ya_ref[...], wa_ref[0], preferred_element_type=F32)
    b = jnp.dot(yb, wb_ref[0], preferred_element_type=F32)
    o = _sigmoid(ga_ref[...].astype(F32)) * a + _sigmoid(gb_ref[...].astype(F32)) * b
    o_ref[...] = o.astype(BF16)


def _merge(ya, yb_ctx, yb_lat, proj, w_pa, w_pb, l, D, tm):
    T = ya.shape[0]
    tn = _tile(D, 512)
    nj = D // tn
    ncb = yb_ctx.shape[0] // tm
    return pl.pallas_call(
        functools.partial(_merge_kernel, ncb=ncb),
        grid=(T // tm, nj),
        in_specs=[pl.BlockSpec((tm, D), lambda i, j: (i, 0)),
                  pl.BlockSpec((tm, D), lambda i, j: (jnp.minimum(i, ncb - 1), 0)),
                  pl.BlockSpec((tm, D), lambda i, j: (jnp.maximum(i - ncb, 0), 0)),
                  pl.BlockSpec((tm, tn), lambda i, j: (i, 5 * nj + j)),
                  pl.BlockSpec((tm, tn), lambda i, j: (i, 6 * nj + j)),
                  pl.BlockSpec((1, D, tn), lambda i, j: (l, 0, j)),
                  pl.BlockSpec((1, D, tn), lambda i, j: (l, 0, j))],
        out_specs=pl.BlockSpec((tm, tn), lambda i, j: (i, j)),
        out_shape=jax.ShapeDtypeStruct((T, D), BF16),
        compiler_params=_params(2, 56),
        name="gated_branch_merge",
    )(ya, yb_ctx, yb_lat, proj, proj, w_pa, w_pb)


def _argmax_first(vals):
    best = vals[0]
    idx = jnp.zeros(best.shape, I32)
    for k in range(1, len(vals)):
        upd = vals[k] > best
        idx = jnp.where(upd, k, idx)
        best = jnp.where(upd, vals[k], best)
    return idx


def _select(idx, vals):
    out = vals[-1]
    for k in range(len(vals) - 2, -1, -1):
        out = jnp.where(idx == k, vals[k], out)
    return out


def _route(logits_t, router_b):
    m = jnp.max(logits_t, axis=0, keepdims=True)
    ex = jnp.exp(logits_t - m)
    probs = ex / jnp.sum(ex, axis=0, keepdims=True)
    sel = probs + router_b
    sel_rows = [sel[e:e + 1, :] for e in range(N_EXPERTS)]
    prob_rows = [probs[e:e + 1, :] for e in range(N_EXPERTS)]
    scores = []
    for g in range(N_EXPERT_GROUPS):
        a, b, c, d = sel_rows[g * EXPERTS_PER_GROUP:(g + 1) * EXPERTS_PER_GROUP]
        hi1, lo1 = jnp.maximum(a, b), jnp.minimum(a, b)
        hi2, lo2 = jnp.maximum(c, d), jnp.minimum(c, d)
        top1 = jnp.maximum(hi1, hi2)
        top2 = jnp.maximum(jnp.minimum(hi1, hi2), jnp.where(hi1 >= hi2, lo1, lo2))
        scores.append(top1 + top2)
    best_g = _argmax_first(scores)
    in_sel = [_select(best_g, [sel_rows[g * EXPERTS_PER_GROUP + k] for g in range(N_EXPERT_GROUPS)])
              for k in range(EXPERTS_PER_GROUP)]
    in_prob = [_select(best_g, [prob_rows[g * EXPERTS_PER_GROUP + k] for g in range(N_EXPERT_GROUPS)])
               for k in range(EXPERTS_PER_GROUP)]
    i1 = _argmax_first(in_sel)
    i2 = _argmax_first([jnp.where(i1 == k, -jnp.inf, in_sel[k]) for k in range(EXPERTS_PER_GROUP)])
    w1 = _select(i1, in_prob)
    w2 = _select(i2, in_prob)
    tot = w1 + w2
    base = best_g * EXPERTS_PER_GROUP
    return (base + i1, base + i2), (w1 / tot, w2 / tot)


def _out_kernel(mg_ref, x_ref, mod_ref, g_ref, w_ref, rw_ref, rb_ref, xo_ref, h_ref, e_ref, gt_ref, acc0, acc1):
    s = pl.program_id(0)

    @pl.when(s == 0)
    def _():
        acc1[...] = jnp.zeros(acc1.shape, acc1.dtype)

    def step(acc_w, acc_r):
        gate1 = mod_ref[0, 0, 2:3, :]
        shift2 = mod_ref[0, 0, 3:4, :]
        scale2 = 1.0 + mod_ref[0, 0, 4:5, :]
        g = g_ref[0]
        acc_w[...] = jnp.dot(mg_ref[...], w_ref[0], preferred_element_type=F32)
        for c in range(x_ref.shape[0] // ROW_CHUNK):
            rows = slice(c * ROW_CHUNK, (c + 1) * ROW_CHUNK)
            xn = x_ref[rows, :] + gate1 * acc_r[rows, :]
            xo_ref[rows, :] = xn
            h_ref[rows, :] = (_rms_scale(xn) * g * scale2 + shift2).astype(BF16)
        logits_t = lax.dot_general(rw_ref[...], h_ref[...], _NT, preferred_element_type=F32)
        experts, gates = _route(logits_t, rb_ref[...])
        for k in range(TOP_K):
            e_ref[k:k + 1, :] = experts[k]
            gt_ref[k:k + 1, :] = gates[k]

    pl.when(s % 2 == 0)(lambda: step(acc0, acc1))
    pl.when(s % 2 == 1)(lambda: step(acc1, acc0))


def _output_projection(merged, x, mod, norm_g, w_out, router_wt, router_b, l, cond_of_block, tm):
    T, D = x.shape
    E = router_wt.shape[0]
    n = T // tm
    cur = lambda s: jnp.minimum(s, n - 1)
    prev = lambda s: jnp.maximum(s - 1, 0)
    return pl.pallas_call(
        _out_kernel,
        grid=(n + 1,),
        in_specs=[pl.BlockSpec((tm, D), lambda s: (cur(s), 0)),
                  pl.BlockSpec((tm, D), lambda s: (prev(s), 0)),
                  pl.BlockSpec((1, 1, N_MOD, D), lambda s: (l, cond_of_block(prev(s)), 0, 0)),
                  pl.BlockSpec((1, 1, D), lambda s: (l, 0, 0)),
                  pl.BlockSpec((1, D, D), lambda s: (l, 0, 0)),
                  pl.BlockSpec((E, D), lambda s: (0, 0)),
                  pl.BlockSpec((E, 1), lambda s: (0, 0))],
        out_specs=[pl.BlockSpec((tm, D), lambda s: (prev(s), 0)),
                   pl.BlockSpec((tm, D), lambda s: (prev(s), 0)),
                   pl.BlockSpec((TOP_K, tm), lambda s: (0, prev(s))),
                   pl.BlockSpec((TOP_K, tm), lambda s: (0, prev(s)))],
        out_shape=[jax.ShapeDtypeStruct((T, D), F32),
                   jax.ShapeDtypeStruct((T, D), BF16),
                   jax.ShapeDtypeStruct((TOP_K, T), I32),
                   jax.ShapeDtypeStruct((TOP_K, T), F32)],
        scratch_shapes=[pltpu.VMEM((tm, D), F32), pltpu.VMEM((tm, D), F32)],
        compiler_params=_params(1, 56),
        name="output_projection_norm2_router",
    )(merged, x, mod, norm_g.reshape(-1, 1, D), w_out, router_wt, router_b.reshape(E, 1))


def _moe_kernel(blk_ref, exp_ref, lo_ref, hi_ref, x_ref, wg_ref, wu_ref, wd_ref, o_ref, wg_b, wu_b, wd_b):
    w = pl.program_id(0)
    lo, hi = lo_ref[w], hi_ref[w]
    row0 = blk_ref[w] * MOE_BLOCK

    @pl.when((w == 0) | (exp_ref[w] != exp_ref[jnp.maximum(w - 1, 0)]))
    def _():
        for src, dst in ((wg_ref, wg_b), (wu_ref, wu_b), (wd_ref, wd_b)):
            def chunk(rows, src=src, dst=dst):
                dst[rows, :] = src[0, rows, :].astype(BF16)

            _for_chunks(dst.shape[0], CAST_CHUNK, chunk)

    @pl.when(lo == row0)
    def _():
        o_ref[...] = jnp.zeros(o_ref.shape, o_ref.dtype)

    @pl.when(hi > lo)
    def _():
        x = x_ref[...]
        g = jnp.dot(x, wg_b[...], preferred_element_type=F32)
        u = jnp.dot(x, wu_b[...], preferred_element_type=F32)
        hid = (g * _sigmoid(g)) * u
        y = jnp.dot(hid.astype(BF16), wd_b[...], preferred_element_type=F32)
        row = row0 + lax.broadcasted_iota(I32, (MOE_BLOCK, 1), 0)
        o_ref[...] += jnp.where((row >= lo) & (row < hi), y, 0.0).astype(o_ref.dtype)


def _moe_experts(xs, items, w_gate, w_up, w_down, l):
    A, D = xs.shape
    E, DE = w_gate.shape[1], w_gate.shape[3]
    w_gate = w_gate.reshape(-1, D, DE)
    w_up = w_up.reshape(-1, D, DE)
    w_down = w_down.reshape(-1, DE, D)
    n_items = items[0].shape[0]
    expert = lambda w, blk, ex, lo, hi: (l * E + ex[w], 0, 0)
    grid_spec = pltpu.PrefetchScalarGridSpec(
        num_scalar_prefetch=4,
        grid=(n_items,),
        in_specs=[pl.BlockSpec((MOE_BLOCK, D), lambda w, blk, ex, lo, hi: (blk[w], 0)),
                  pl.BlockSpec((1, D, DE), expert, pipeline_mode=pl.Buffered(1)),
                  pl.BlockSpec((1, D, DE), expert, pipeline_mode=pl.Buffered(1)),
                  pl.BlockSpec((1, DE, D), expert)],
        out_specs=pl.BlockSpec((MOE_BLOCK, D), lambda w, blk, ex, lo, hi: (blk[w], 0)),
        scratch_shapes=[pltpu.VMEM((D, DE), BF16), pltpu.VMEM((D, DE), BF16), pltpu.VMEM((DE, D), BF16)],
    )
    return pl.pallas_call(
        _moe_kernel,
        grid_spec=grid_spec,
        out_shape=jax.ShapeDtypeStruct((A, D), BF16),
        compiler_params=_params(1, 58),
        name="moe_experts",
    )(*items, xs, w_gate, w_up, w_down)


def _dispatch_plan(experts, T):
    A = TOP_K * T
    nb = A // MOE_BLOCK
    n_items = nb + N_EXPERTS - 1
    e_flat = experts.reshape(A)
    iota = jnp.arange(A, dtype=I32)
    _, order = lax.sort((e_flat, iota), num_keys=1, is_stable=True)
    _, rank = lax.sort((order, iota), num_keys=1)
    eid = jnp.arange(N_EXPERTS, dtype=I32)
    counts = jnp.sum((e_flat[:, None] == eid[None, :]).astype(I32), axis=0)
    ends = jnp.cumsum(counts)
    starts = ends - counts
    first_blk = starts // MOE_BLOCK
    n_blk = jnp.where(counts > 0, (ends - 1) // MOE_BLOCK - first_blk + 1, 0)
    item_end = jnp.cumsum(n_blk)
    item_start = item_end - n_blk
    w = jnp.arange(n_items, dtype=I32)
    valid = w < item_end[-1]
    ex = jnp.minimum(jnp.sum((w[:, None] >= item_end[None, :]).astype(I32), axis=1), N_EXPERTS - 1)
    pick = lambda tbl: jnp.sum(jnp.where(ex[:, None] == eid[None, :], tbl[None, :], 0), axis=1)
    blk = jnp.where(valid, pick(first_blk) + w - pick(item_start), nb - 1)
    lo = jnp.where(valid, jnp.maximum(pick(starts), blk * MOE_BLOCK), 0)
    hi = jnp.where(valid, jnp.minimum(pick(ends), (blk + 1) * MOE_BLOCK), 0)
    ex = jnp.where(valid, ex, jnp.max(jnp.where(counts > 0, eid, 0)))
    return order % T, rank, (blk.astype(I32), ex.astype(I32), lo.astype(I32), hi.astype(I32))


def _combined(x_ref, y1_ref, y2_ref, g1_ref, g2_ref, gate2, rows):
    y = g1_ref[rows, :] * y1_ref[rows, :].astype(F32) + g2_ref[rows, :] * y2_ref[rows, :].astype(F32)
    return x_ref[rows, :] + gate2 * y


def _combine_kernel(x_ref, y1_ref, y2_ref, g1_ref, g2_ref, mod_ref, modn_ref, gn_ref, xo_ref, h_ref):
    gate2 = mod_ref[0, 0, 5:6, :]
    scale1 = 1.0 + modn_ref[0, 0, 1:2, :]
    shift1 = modn_ref[0, 0, 0:1, :]
    g = gn_ref[0]

    def chunk(rows):
        xn = _combined(x_ref, y1_ref, y2_ref, g1_ref, g2_ref, gate2, rows)
        xo_ref[rows, :] = xn
        h_ref[rows, :] = (_rms_scale(xn) * g * scale1 + shift1).astype(BF16)

    _for_chunks(x_ref.shape[0], ROW_CHUNK, chunk)


def _combine_final_kernel(x_ref, y1_ref, y2_ref, g1_ref, g2_ref, mod_ref, gf_ref, yp_ref, ys_ref, *, ncb):
    i = pl.program_id(0)
    gate2 = mod_ref[0, 0, 5:6, :]
    g = gf_ref[...]

    def run(dst_ref):
        def chunk(rows):
            xn = _combined(x_ref, y1_ref, y2_ref, g1_ref, g2_ref, gate2, rows)
            dst_ref[rows, :] = _rms_scale(xn) * g

        _for_chunks(x_ref.shape[0], ROW_CHUNK, chunk)

    pl.when(i < ncb)(lambda: run(yp_ref))
    pl.when(i >= ncb)(lambda: run(ys_ref))


def _combine(x, y1, y2, g1, g2, mod, norm1_g, l, cond_of_block, tm):
    T, D = x.shape
    row = pl.BlockSpec((tm, D), lambda i: (i, 0))
    col = pl.BlockSpec((tm, 1), lambda i: (i, 0))
    return pl.pallas_call(
        _combine_kernel,
        grid=(T // tm,),
        in_specs=[row, row, pl.BlockSpec((tm, D), lambda i: (T // tm + i, 0)), col, col,
                  pl.BlockSpec((1, 1, N_MOD, D), lambda i: (l, cond_of_block(i), 0, 0)),
                  pl.BlockSpec((1, 1, N_MOD, D), lambda i: (l + 1, cond_of_block(i), 0, 0)),
                  pl.BlockSpec((1, 1, D), lambda i: (l + 1, 0, 0))],
        out_specs=[row, row],
        out_shape=[jax.ShapeDtypeStruct((T, D), F32), jax.ShapeDtypeStruct((T, D), BF16)],
        compiler_params=_params(1, 52),
        name="moe_combine_residual_norm1",
    )(x, y1, y2, g1, g2, mod, mod, norm1_g.reshape(-1, 1, D))


def _combine_final(x, y1, y2, g1, g2, mod, final_g, l, nc, cond_of_block, tm):
    T, D = x.shape
    ncb = nc // tm
    row = pl.BlockSpec((tm, D), lambda i: (i, 0))
    col = pl.BlockSpec((tm, 1), lambda i: (i, 0))
    return pl.pallas_call(
        functools.partial(_combine_final_kernel, ncb=ncb),
        grid=(T // tm,),
        in_specs=[row, row, pl.BlockSpec((tm, D), lambda i: (T // tm + i, 0)), col, col,
                  pl.BlockSpec((1, 1, N_MOD, D), lambda i: (l, cond_of_block(i), 0, 0)),
                  pl.BlockSpec((1, D), lambda i: (0, 0))],
        out_specs=[pl.BlockSpec((tm, D), lambda i: (jnp.minimum(i, ncb - 1), 0)),
                   pl.BlockSpec((tm, D), lambda i: (jnp.maximum(i - ncb, 0), 0))],
        out_shape=[jax.ShapeDtypeStruct((nc, D), F32), jax.ShapeDtypeStruct((T - nc, D), F32)],
        compiler_params=_params(1, 52),
        name="moe_combine_residual_final_norm",
    )(x, y1, y2, g1, g2, mod, final_g.reshape(1, D))


def kernel(x_prompt, x_sample, cache_k, cache_v, c, c_ctx, w_ada, b_ada, norm1_g, w_in, gmlp_g, w_s, b_s, rpb,
           w_proj_a, w_proj_b, w_out, norm2_g, router_w, router_b, w_gate, w_up, w_down, final_g):
    batch, seq, D = x_prompt.shape
    dec_batch, dec_seq, _ = x_sample.shape
    depth = w_in.shape[0]
    heads, dh = cache_k.shape[3], cache_k.shape[4]
    past = cache_k.shape[2]
    nc, nl = batch * seq, dec_batch * dec_seq
    T = nc + nl
    assert heads * dh == D and gmlp_g.shape[1] == D and w_in.shape[2] == 7 * D
    assert 1 + dec_batch <= COND_ROWS
    assert router_w.shape[1] == N_EXPERTS
    assert seq % GMLP_CHUNK == 0 and dec_seq % GMLP_CHUNK == 0 and dec_seq % GRID_W == 0
    assert nc % dec_seq == 0 and (TOP_K * T) % MOE_BLOCK == 0

    tm_big = 1024 if (nc % 1024 == 0 and dec_seq % 1024 == 0) else GMLP_CHUNK
    tm_mid = 512 if (nc % 512 == 0 and dec_seq % 512 == 0) else GMLP_CHUNK

    def cond_of(tm):
        ncb, per = nc // tm, dec_seq // tm
        return lambda i: jnp.where(i < ncb, 0, 1 + (i - ncb) // per)

    cond = jnp.zeros((COND_ROWS, D), F32).at[0].set(c_ctx).at[1:1 + dec_batch].set(c)
    mod = _modulation_all(cond, w_ada, b_ada)

    w_s_b = w_s.astype(BF16)
    w_pa_b = w_proj_a.astype(BF16)
    w_pb_b = w_proj_b.astype(BF16)
    w_out_b = w_out.astype(BF16)
    kc_all = cache_k.astype(BF16).reshape(dec_batch, depth, past, D)
    vc_all = cache_v.astype(BF16).reshape(dec_batch, depth, past, D)
    router_wt = router_w.T.astype(BF16)
    na_tables = _na_tables(rpb, dec_seq // GRID_W)

    x, h = _prologue(x_prompt.reshape(nc, D), x_sample.reshape(nl, D), mod, norm1_g, cond_of(tm_mid), tm_mid)
    new_k = jnp.zeros((batch, depth, seq, D), F32)
    new_v = jnp.zeros((batch, depth, seq, D), F32)
    for l in range(depth):
        proj, new_k, new_v = _input_projection(h, w_in, new_k, new_v, l, tm_big, dh ** -0.5 * LOG2E)
        ya = _gmlp(proj, gmlp_g, w_s_b, b_s, l, D, tm_mid)
        yb_ctx = _context_attention(proj, batch, seq, D, heads, dh)
        yb_lat = _neighbourhood_attention(proj, kc_all[:, l], vc_all[:, l], na_tables, l, nc, dec_batch,
                                          dec_seq, D, heads, dh)
        merged = _merge(ya, yb_ctx, yb_lat, proj, w_pa_b, w_pb_b, l, D, tm_big)
        x, h2, experts, gates = _output_projection(merged, x, mod, norm2_g, w_out_b, router_wt, router_b, l,
                                                   cond_of(tm_mid), tm_mid)
        tok_sorted, rank, items = _dispatch_plan(experts, T)
        yexp = _moe_experts(h2[tok_sorted], items, w_gate, w_up, w_down, l)
        y1 = y2 = yexp[rank]
        g1, g2 = gates[0][:, None], gates[1][:, None]
        if l + 1 < depth:
            x, h = _combine(x, y1, y2, g1, g2, mod, norm1_g, l, cond_of(tm_mid), tm_mid)
        else:
            y_prompt, y_sample = _combine_final(x, y1, y2, g1, g2, mod, final_g, l, nc, cond_of(tm_mid), tm_mid)

    return (y_prompt.reshape(batch, seq, D), y_sample.reshape(dec_batch, dec_seq, D),
            new_k.reshape(batch, depth, seq, heads, dh), new_v.reshape(batch, depth, seq, heads, dh))
```
